```python
import math, functools
import jax, jax.numpy as jnp
from jax import lax
import numpy as np

D_MODEL = 1024
BATCH = 8
SEQ = 2048
DEPTH = 1
DEC_BATCH = 32
DEC_SEQ = 4
PAST_LEN = 16384
PAGE_SIZE = 128

SSD_D_INNER = 2 * D_MODEL
SSD_HEAD_DIM = 64
SSD_N_HEADS = SSD_D_INNER // SSD_HEAD_DIM
SSD_N_GROUPS = 8
SSD_D_STATE = 128
SSD_CONV_W = 4
SSD_CONV_DIM = SSD_D_INNER + 2 * SSD_N_GROUPS * SSD_D_STATE
SSD_CHUNK = 128
ATT_N_HEADS = 16
ATT_N_KV_HEADS = 4
ATT_HEAD_DIM = 64
IDX_N_HEADS = 8
IDX_HEAD_DIM = 64
TOPK_MAX = 256
Q_BLOCK = 128
MEM_LEN = 256
MEM_N_HEADS = 4
MEM_HEAD_DIM = D_MODEL // MEM_N_HEADS
FFN_HIDDEN = 2816
EPS = 1e-6

IN_SPLITS = (SSD_D_INNER, SSD_CONV_DIM, SSD_N_HEADS,
             ATT_N_HEADS * ATT_HEAD_DIM, ATT_N_KV_HEADS * ATT_HEAD_DIM, ATT_N_KV_HEADS * ATT_HEAD_DIM,
             IDX_N_HEADS * IDX_HEAD_DIM, IDX_HEAD_DIM, IDX_N_HEADS,
             D_MODEL, D_MODEL)
IN_DIM = sum(IN_SPLITS)
IN_OFFSETS = tuple(int(o) for o in np.cumsum(IN_SPLITS)[:-1])

kernel_name = 'hybrid_ssd_dsa_macaron_step'


def rmsnorm(x, g):
    xf = x.astype(jnp.float32)
    y = xf * lax.rsqrt(jnp.mean(xf * xf, axis=-1, keepdims=True) + EPS)
    return (y * g.astype(jnp.float32)).astype(x.dtype)


def swiglu(x, wg, wu, wd):
    return (jax.nn.silu(x @ wg) * (x @ wu)) @ wd


def gather_rows(a, idx):
    return jax.vmap(lambda ar, ix: ar[ix])(a, idx)


def causal_dwconv(xbc, buf, w, b):
    l = xbc.shape[1]
    xp = jnp.concatenate([buf.astype(xbc.dtype), xbc], axis=1)
    y = b
    for j in range(SSD_CONV_W):
        y = y + w[j] * xp[:, j:j + l]
    return jax.nn.silu(y), xp[:, l:]


def ssd_scan(x, dt, a, bm, cm, h0):
    b, l, h, p = x.shape
    g, n = bm.shape[2], bm.shape[3]
    r = h // g
    q = math.gcd(l, SSD_CHUNK)
    c = l // q
    xdt = (x * dt[..., None]).reshape(b, c, q, g, r, p)
    a_cs = jnp.cumsum((dt * a).reshape(b, c, q, g, r), axis=2)
    bc = bm.reshape(b, c, q, g, n)
    cc = cm.reshape(b, c, q, g, n)
    causal = jnp.tril(jnp.ones((q, q), dtype=bool))[:, :, None, None]
    seg = a_cs[:, :, :, None] - a_cs[:, :, None, :]
    decay = jnp.exp(jnp.where(causal, seg, -jnp.inf))
    cb = jnp.einsum('bctgn,bcsgn->bctsg', cc, bc)
    y_diag = jnp.einsum('bctsgr,bcsgrp->bctgrp', cb[..., None] * decay, xdt)
    to_end = jnp.exp(a_cs[:, :, -1:] - a_cs)
    chunk_states = jnp.einsum('bcsgn,bcsgrp->bcgrpn', bc, xdt * to_end[..., None])
    chunk_decay = jnp.exp(a_cs[:, :, -1])

    def step(state, inp):
        s_c, d_c = inp
        return state * d_c[..., None, None] + s_c, state

    h_fin, h_in = lax.scan(step, h0.reshape(b, g, r, p, n),
                           (jnp.moveaxis(chunk_states, 1, 0), jnp.moveaxis(chunk_decay, 1, 0)))
    h_in = jnp.moveaxis(h_in, 0, 1)
    y_off = jnp.einsum('bctgn,bcgrpn->bctgrp', cc, h_in) * jnp.exp(a_cs)[..., None]
    y = (y_diag + y_off).reshape(b, l, h, p)
    return y, h_fin.reshape(b, h, p, n)


def ssd_branch(z, xbc, dt_raw, conv_buf, h0, conv_w, conv_b, dt_bias, a_log, d_skip, norm_g):
    f32 = jnp.float32
    b, l, _ = z.shape
    xbc_act, new_buf = causal_dwconv(xbc, conv_buf, conv_w, conv_b)
    xs, bm, cm = jnp.split(xbc_act, [SSD_D_INNER, SSD_D_INNER + SSD_N_GROUPS * SSD_D_STATE], axis=-1)
    xs = xs.reshape(b, l, SSD_N_HEADS, SSD_HEAD_DIM).astype(f32)
    bm = bm.reshape(b, l, SSD_N_GROUPS, SSD_D_STATE).astype(f32)
    cm = cm.reshape(b, l, SSD_N_GROUPS, SSD_D_STATE).astype(f32)
    dt = jax.nn.softplus((dt_raw + dt_bias).astype(f32))
    a = -jnp.exp(a_log.astype(f32))
    y, h_new = ssd_scan(xs, dt, a, bm, cm, h0.astype(f32))
    y = y + d_skip.astype(f32)[:, None] * xs
    y = y.reshape(b, l, SSD_D_INNER) * jax.nn.silu(z.astype(f32))
    yg = y.reshape(b, l, SSD_N_GROUPS, -1)
    yg = yg * lax.rsqrt(jnp.mean(yg * yg, axis=-1, keepdims=True) + EPS)
    y = yg.reshape(b, l, SSD_D_INNER) * norm_g.astype(f32)
    return y.astype(z.dtype), new_buf, h_new.astype(h0.dtype)


def indexer_select(qi, wi, ki, q_pos, n_keys, topk):
    dots = jnp.einsum('bthd,bsd->bths', qi, ki).astype(jnp.float32) * (IDX_HEAD_DIM ** -0.5)
    score = jnp.einsum('bth,bths->bts', wi.astype(jnp.float32) * (IDX_N_HEADS ** -0.5), jax.nn.relu(dots))
    allowed = jnp.arange(n_keys)[None, :] <= q_pos[:, None]
    score = jnp.where(allowed[None], score, -jnp.inf)
    _, sel = lax.top_k(score, topk)
    valid = sel <= q_pos[None, :, None]
    return sel, valid


def sparse_attend(q, k_sel, v_sel, valid):
    b, t = q.shape[:2]
    qg = q.reshape(b, t, ATT_N_KV_HEADS, ATT_N_HEADS // ATT_N_KV_HEADS, ATT_HEAD_DIM)
    s = jnp.einsum('btgrd,btkgd->btgrk', qg, k_sel).astype(jnp.float32) * (ATT_HEAD_DIM ** -0.5)
    s = jnp.where(valid[:, :, None, None, :], s, -jnp.inf)
    pr = jax.nn.softmax(s, axis=-1).astype(v_sel.dtype)
    o = jnp.einsum('btgrk,btkgd->btgrd', pr, v_sel)
    return o.reshape(b, t, ATT_N_HEADS * ATT_HEAD_DIM)


def dsa_prompt(q, k, v, qi, ki, wi):
    b, T = q.shape[:2]
    topk = min(TOPK_MAX, T // 4)

    def block(t0):
        sl = lambda arr: lax.dynamic_slice_in_dim(arr, t0, Q_BLOCK, axis=1)
        q_pos = t0 + jnp.arange(Q_BLOCK)
        sel, valid = indexer_select(sl(qi), sl(wi), ki, q_pos, T, topk)
        return sparse_attend(sl(q), gather_rows(k, sel), gather_rows(v, sel), valid)

    out = lax.map(block, jnp.arange(T // Q_BLOCK) * Q_BLOCK)
    return jnp.moveaxis(out, 0, 1).reshape(b, T, ATT_N_HEADS * ATT_HEAD_DIM)


def dsa_sample(q, k, v, qi, ki, wi, pool_k, pool_v, pool_kidx, page_table):
    b, l = q.shape[:2]
    n_pages = page_table.shape[1]
    past = n_pages * PAGE_SIZE
    n_keys = past + l
    topk = min(TOPK_MAX, n_keys // 4)
    ki_past = pool_kidx[page_table].reshape(b, past, IDX_HEAD_DIM)
    ki_all = jnp.concatenate([ki_past.astype(ki.dtype), ki], axis=1)
    q_pos = past + jnp.arange(l)
    sel, valid = indexer_select(qi, wi, ki_all, q_pos, n_keys, topk)
    is_past = sel < past
    ps = jnp.where(is_past, sel, 0)
    phys = jax.vmap(lambda pt, s: pt[s])(page_table, ps // PAGE_SIZE)
    off = ps % PAGE_SIZE
    ns = jnp.clip(sel - past, 0, l - 1)
    m = is_past[..., None, None]
    k_sel = jnp.where(m, pool_k[phys, off].astype(k.dtype), gather_rows(k, ns))
    v_sel = jnp.where(m, pool_v[phys, off].astype(v.dtype), gather_rows(v, ns))
    return sparse_attend(q, k_sel, v_sel, valid)


def memory_kv(mem, g, wk, wv):
    b, n, _ = mem.shape
    m = rmsnorm(mem, g)
    return ((m @ wk).reshape(b, n, MEM_N_HEADS, MEM_HEAD_DIM),
            (m @ wv).reshape(b, n, MEM_N_HEADS, MEM_HEAD_DIM))


def mem_attend(u, wq, mk, mv):
    b, l, _ = u.shape
    q = (u @ wq).reshape(b, l, MEM_N_HEADS, MEM_HEAD_DIM)
    s = jnp.einsum('blhd,bmhd->bhlm', q, mk.astype(q.dtype)).astype(jnp.float32) * (MEM_HEAD_DIM ** -0.5)
    pr = jax.nn.softmax(s, axis=-1).astype(q.dtype)
    return jnp.einsum('bhlm,bmhd->blhd', pr, mv.astype(q.dtype)).reshape(b, l, MEM_N_HEADS * MEM_HEAD_DIM)


def layer_forward(h, lp, conv_buf, ssm_h0, attn_fn, mem_k, mem_v):
    u = rmsnorm(h, lp['ffn1_pre_g'])
    h = h + 0.5 * rmsnorm(swiglu(u, lp['ffn1_wg'], lp['ffn1_wu'], lp['ffn1_wd']), lp['ffn1_post_g'])
    u = rmsnorm(h, lp['mix_pre_g'])
    b, l, _ = u.shape
    z, xbc, dt_raw, q, k, v, qi, ki, wi, g_ssd, g_att = jnp.split(u @ lp['w_in'], IN_OFFSETS, axis=-1)
    y_ssd, new_conv, new_ssm = ssd_branch(z, xbc, dt_raw, conv_buf, ssm_h0, lp['conv_w'], lp['conv_b'],
                                          lp['dt_bias'], lp['a_log'], lp['d_skip'], lp['ssd_norm_g'])
    q = q.reshape(b, l, ATT_N_HEADS, ATT_HEAD_DIM)
    k = k.reshape(b, l, ATT_N_KV_HEADS, ATT_HEAD_DIM)
    v = v.reshape(b, l, ATT_N_KV_HEADS, ATT_HEAD_DIM)
    qi = qi.reshape(b, l, IDX_N_HEADS, IDX_HEAD_DIM)
    y_att = attn_fn(q, k, v, qi, ki, wi)
    merged = (jax.nn.sigmoid(g_ssd) * (y_ssd @ lp['w_br_ssd'])
              + jax.nn.sigmoid(g_att) * (y_att @ lp['w_br_att']))
    h = h + rmsnorm(merged @ lp['w_out'], lp['mix_post_g'])
    u = rmsnorm(h, lp['mem_pre_g'])
    h = h + rmsnorm(mem_attend(u, lp['w_mq'], mem_k, mem_v) @ lp['w_mo'], lp['mem_post_g'])
    u = rmsnorm(h, lp['ffn2_pre_g'])
    h = h + 0.5 * rmsnorm(swiglu(u, lp['ffn2_wg'], lp['ffn2_wu'], lp['ffn2_wd']), lp['ffn2_post_g'])
    return h, (k, v, ki, new_ssm, new_conv)


def setup_inputs(seed: int = 0) -> dict:
    key = jax.random.key(seed)
    ks = iter(jax.random.split(key, 64))

    def nrm(shape, scale=1.0):
        return jax.random.normal(next(ks), shape, jnp.float32) * scale

    def gain(n):
        return 1.0 + nrm((DEPTH, n), 0.02)

    n_pages = PAST_LEN // PAGE_SIZE
    n_pool = (DEC_BATCH * n_pages * 5) // 4
    D = D_MODEL
    inp = {}
    inp['x_prompt'] = nrm((BATCH, SEQ, D))
    inp['x_sample'] = nrm((DEC_BATCH, DEC_SEQ, D))
    inp['mem_prompt'] = nrm((BATCH, MEM_LEN, D))
    inp['cache_k'] = nrm((DEPTH, n_pool, PAGE_SIZE, ATT_N_KV_HEADS, ATT_HEAD_DIM))
    inp['cache_v'] = nrm((DEPTH, n_pool, PAGE_SIZE, ATT_N_KV_HEADS, ATT_HEAD_DIM))
    inp['cache_kidx'] = nrm((DEPTH, n_pool, PAGE_SIZE, IDX_HEAD_DIM))
    inp['state_ssm'] = nrm((DEPTH, DEC_BATCH, SSD_N_HEADS, SSD_HEAD_DIM, SSD_D_STATE), 0.3)
    inp['state_conv'] = nrm((DEPTH, DEC_BATCH, SSD_CONV_W - 1, SSD_CONV_DIM))
    inp['cache_mem_k'] = nrm((DEPTH, DEC_BATCH, MEM_LEN, MEM_N_HEADS, MEM_HEAD_DIM))
    inp['cache_mem_v'] = nrm((DEPTH, DEC_BATCH, MEM_LEN, MEM_N_HEADS, MEM_HEAD_DIM))
    perm = jax.random.permutation(next(ks), n_pool)[: DEC_BATCH * n_pages]
    inp['page_table'] = perm.reshape(DEC_BATCH, n_pages).astype(jnp.int32)
    inp['ffn1_pre_g'] = gain(D)
    inp['ffn1_wg'] = nrm((DEPTH, D, FFN_HIDDEN), D ** -0.5)
    inp['ffn1_wu'] = nrm((DEPTH, D, FFN_HIDDEN), D ** -0.5)
    inp['ffn1_wd'] = nrm((DEPTH, FFN_HIDDEN, D), FFN_HIDDEN ** -0.5)
    inp['ffn1_post_g'] = gain(D)
    inp['mix_pre_g'] = gain(D)
    inp['w_in'] = nrm((DEPTH, D, IN_DIM), D ** -0.5)
    inp['conv_w'] = nrm((DEPTH, SSD_CONV_W, SSD_CONV_DIM), 0.5)
    inp['conv_b'] = nrm((DEPTH, SSD_CONV_DIM), 0.01)
    u = jax.random.uniform(next(ks), (DEPTH, SSD_N_HEADS))
    dt0 = jnp.exp(u * (math.log(0.1) - math.log(0.001)) + math.log(0.001))
    inp['dt_bias'] = dt0 + jnp.log(-jnp.expm1(-dt0))
    inp['a_log'] = jnp.log(jax.random.uniform(next(ks), (DEPTH, SSD_N_HEADS), minval=1.0, maxval=16.0))
    inp['d_skip'] = 1.0 + nrm((DEPTH, SSD_N_HEADS), 0.1)
    inp['ssd_norm_g'] = gain(SSD_D_INNER)
    inp['w_br_ssd'] = nrm((DEPTH, SSD_D_INNER, D), SSD_D_INNER ** -0.5)
    inp['w_br_att'] = nrm((DEPTH, ATT_N_HEADS * ATT_HEAD_DIM, D), (ATT_N_HEADS * ATT_HEAD_DIM) ** -0.5)
    inp['w_out'] = nrm((DEPTH, D, D), D ** -0.5)
    inp['mix_post_g'] = gain(D)
    inp['mem_pre_g'] = gain(D)
    inp['mem_kv_g'] = gain(D)
    inp['w_mq'] = nrm((DEPTH, D, MEM_N_HEADS * MEM_HEAD_DIM), D ** -0.5)
    inp['w_mk'] = nrm((DEPTH, D, MEM_N_HEADS * MEM_HEAD_DIM), D ** -0.5)
    inp['w_mv'] = nrm((DEPTH, D, MEM_N_HEADS * MEM_HEAD_DIM), D ** -0.5)
    inp['w_mo'] = nrm((DEPTH, MEM_N_HEADS * MEM_HEAD_DIM, D), (MEM_N_HEADS * MEM_HEAD_DIM) ** -0.5)
    inp['mem_post_g'] = gain(D)
    inp['ffn2_pre_g'] = gain(D)
    inp['ffn2_wg'] = nrm((DEPTH, D, FFN_HIDDEN), D ** -0.5)
    inp['ffn2_wu'] = nrm((DEPTH, D, FFN_HIDDEN), D ** -0.5)
    inp['ffn2_wd'] = nrm((DEPTH, FFN_HIDDEN, D), FFN_HIDDEN ** -0.5)
    inp['ffn2_post_g'] = gain(D)
    return inp


def reference(x_prompt, x_sample, mem_prompt, cache_k, cache_v, cache_kidx, state_ssm, state_conv,
              cache_mem_k, cache_mem_v, page_table,
              ffn1_pre_g, ffn1_wg, ffn1_wu, ffn1_wd, ffn1_post_g,
              mix_pre_g, w_in, conv_w, conv_b, dt_bias, a_log, d_skip, ssd_norm_g,
              w_br_ssd, w_br_att, w_out, mix_post_g,
              mem_pre_g, mem_kv_g, w_mq, w_mk, w_mv, w_mo, mem_post_g,
              ffn2_pre_g, ffn2_wg, ffn2_wu, ffn2_wd, ffn2_post_g):
    y_p, y_s = x_prompt, x_sample
    b_p = x_prompt.shape[0]
    acc = [[] for _ in range(12)]
    for i in range(DEPTH):
        lp = dict(ffn1_pre_g=ffn1_pre_g[i], ffn1_wg=ffn1_wg[i], ffn1_wu=ffn1_wu[i], ffn1_wd=ffn1_wd[i],
                  ffn1_post_g=ffn1_post_g[i], mix_pre_g=mix_pre_g[i], w_in=w_in[i], conv_w=conv_w[i],
                  conv_b=conv_b[i], dt_bias=dt_bias[i], a_log=a_log[i], d_skip=d_skip[i],
                  ssd_norm_g=ssd_norm_g[i], w_br_ssd=w_br_ssd[i], w_br_att=w_br_att[i], w_out=w_out[i],
                  mix_post_g=mix_post_g[i], mem_pre_g=mem_pre_g[i], w_mq=w_mq[i], w_mo=w_mo[i],
                  mem_post_g=mem_post_g[i], ffn2_pre_g=ffn2_pre_g[i], ffn2_wg=ffn2_wg[i], ffn2_wu=ffn2_wu[i],
                  ffn2_wd=ffn2_wd[i], ffn2_post_g=ffn2_post_g[i])
        mk_p, mv_p = memory_kv(mem_prompt, mem_kv_g[i], w_mk[i], w_mv[i])
        conv0 = jnp.zeros((b_p, SSD_CONV_W - 1, SSD_CONV_DIM), x_prompt.dtype)
        h0 = jnp.zeros((b_p, SSD_N_HEADS, SSD_HEAD_DIM, SSD_D_STATE), jnp.float32)
        y_p, (k_p, v_p, ki_p, ssm_p, conv_p) = layer_forward(y_p, lp, conv0, h0, dsa_prompt, mk_p, mv_p)
        samp_attn = functools.partial(dsa_sample, pool_k=cache_k[i], pool_v=cache_v[i],
                                      pool_kidx=cache_kidx[i], page_table=page_table)
        y_s, (k_s, v_s, ki_s, ssm_s, conv_s) = layer_forward(y_s, lp, state_conv[i], state_ssm[i], samp_attn,
                                                             cache_mem_k[i], cache_mem_v[i])
        for lst, val in zip(acc, (k_p, v_p, ki_p, ssm_p, conv_p, mk_p, mv_p, k_s, v_s, ki_s, ssm_s, conv_s)):
            lst.append(val)
    (nk_p, nv_p, nki_p, nssm_p, nconv_p, nmk_p, nmv_p,
     nk_s, nv_s, nki_s, nssm_s, nconv_s) = [jnp.stack(lst) for lst in acc]
    return (y_p, y_s, nk_p, nv_p, nki_p, nssm_p, nconv_p, nmk_p, nmv_p, nk_s, nv_s, nki_s, nssm_s, nconv_s)
```

```python
import functools
import math

import jax
import jax.numpy as jnp
from jax import lax
from jax.experimental import pallas as pl
from jax.experimental.pallas import tpu as pltpu

F32 = jnp.float32
BF16 = jnp.bfloat16
I32 = jnp.int32

D_MODEL = 1024
EPS = 1e-6
FFN_HIDDEN = 2816
SSD_D_INNER = 2048
SSD_HEAD_DIM = 64
SSD_N_HEADS = 32
SSD_N_GROUPS = 8
SSD_D_STATE = 128
SSD_CONV_W = 4
SSD_CONV_DIM = 4096
SSD_CHUNK = 128
SSD_GROUP_W = SSD_D_INNER // SSD_N_GROUPS
SSD_HEADS_PER_GROUP = SSD_N_HEADS // SSD_N_GROUPS
ATT_N_HEADS = 16
ATT_N_KV_HEADS = 4
ATT_HEAD_DIM = 64
ATT_REP = ATT_N_HEADS // ATT_N_KV_HEADS
ATT_KV_W = ATT_N_KV_HEADS * ATT_HEAD_DIM
IDX_N_HEADS = 8
IDX_HEAD_DIM = 64
TOPK_MAX = 256
Q_BLOCK = 128
PAGE_SIZE = 128
MEM_LEN = 256
MEM_N_HEADS = 4
MEM_HEAD_DIM = 256

P_XBC = 0
P_Z = 4096
P_Q = 6144
P_K = 7168
P_V = 7424
P_QI = 7680
P_GSSD = 8192
P_GATT = 9216
P_SMALL = 10240
P_WIDTH = 10368
SM_DT = 0
SM_KI = 32
SM_WI = 96
LANES = 128
SUBLANES = 8

SAMPLE_PAD = 16
SAMPLE_ROWS = 8

VMEM_LIMIT = 56 * 1024 * 1024
NEG_BIG = -1e30
INT_MIN = -2147483648


def _cparams(sem):
    return pltpu.CompilerParams(dimension_semantics=sem, vmem_limit_bytes=VMEM_LIMIT)


def _rms(x, g):
    return x * lax.rsqrt(jnp.mean(x * x, axis=-1, keepdims=True) + EPS) * g


def _sigmoid(x):
    return 1.0 / (1.0 + jnp.exp(-x))


def _silu(x):
    return x * _sigmoid(x)


def _dot(a, b):
    return jnp.dot(a, b, preferred_element_type=F32)


def _dot_nt(a, b):
    return lax.dot_general(a, b, (((1,), (1,)), ((), ())), preferred_element_type=F32)


def _split_bf16(x, parts):
    out = []
    r = x
    for _ in range(parts):
        p = r.astype(BF16)
        out.append(p)
        r = r - p.astype(F32)
    return out


def _ffn_body(x_ref, pg_ref, wg_ref, wu_ref, wd_ref, qg_ref, o_ref, xn_ref, acc_ref):
    j = pl.program_id(1)

    @pl.when(j == 0)
    def _():
        xn_ref[...] = _rms(x_ref[...], pg_ref[...]).astype(BF16)
        acc_ref[...] = jnp.zeros_like(acc_ref)

    xn = xn_ref[...]
    g = _dot(xn, wg_ref[...])
    u = _dot(xn, wu_ref[...])
    a = _silu(g) * u
    acc_ref[...] += _dot(a.astype(BF16), wd_ref[...])

    @pl.when(j == pl.num_programs(1) - 1)
    def _():
        o_ref[...] = x_ref[...] + 0.5 * _rms(acc_ref[...], qg_ref[...])


def _ffn(x, pre_g, wg, wu, wd, post_g, *, tm, th=256):
    m = x.shape[0]
    return pl.pallas_call(
        _ffn_body,
        grid=(m // tm, FFN_HIDDEN // th),
        in_specs=[
            pl.BlockSpec((tm, D_MODEL), lambda i, j: (i, 0)),
            pl.BlockSpec((1, D_MODEL), lambda i, j: (0, 0)),
            pl.BlockSpec((D_MODEL, th), lambda i, j: (0, j)),
            pl.BlockSpec((D_MODEL, th), lambda i, j: (0, j)),
            pl.BlockSpec((th, D_MODEL), lambda i, j: (j, 0)),
            pl.BlockSpec((1, D_MODEL), lambda i, j: (0, 0)),
        ],
        out_specs=pl.BlockSpec((tm, D_MODEL), lambda i, j: (i, 0)),
        out_shape=jax.ShapeDtypeStruct((m, D_MODEL), F32),
        scratch_shapes=[pltpu.VMEM((tm, D_MODEL), BF16), pltpu.VMEM((tm, D_MODEL), F32)],
        compiler_params=_cparams(("parallel", "arbitrary")),
        name="ffn",
    )(x, pre_g, wg, wu, wd, post_g)


def _proj_body(x_ref, g_ref, w_ref, o_ref, xn_ref):
    @pl.when(pl.program_id(1) == 0)
    def _():
        xn_ref[...] = _rms(x_ref[...], g_ref[...]).astype(BF16)

    o_ref[...] = _dot(xn_ref[...], w_ref[...])


def _norm_proj(x, g, w, *, tm, tn):
    m, n = x.shape[0], w.shape[1]
    return pl.pallas_call(
        _proj_body,
        grid=(m // tm, n // tn),
        in_specs=[
            pl.BlockSpec((tm, D_MODEL), lambda i, j: (i, 0)),
            pl.BlockSpec((1, D_MODEL), lambda i, j: (0, 0)),
            pl.BlockSpec((D_MODEL, tn), lambda i, j: (0, j)),
        ],
        out_specs=pl.BlockSpec((tm, tn), lambda i, j: (i, j)),
        out_shape=jax.ShapeDtypeStruct((m, n), F32),
        scratch_shapes=[pltpu.VMEM((tm, D_MODEL), BF16)],
        compiler_params=_cparams(("parallel", "arbitrary")),
        name="norm_proj",
    )(x, g, w)


XP_BASE = 8


def _ssd_body(xbc_ref, z_ref, sm_ref, conv0_ref, ssm0_ref, cw_ref, cb_ref, dtb_ref, alog_ref, dsk_ref,
              ng_ref, tri_ref, exp_ref,
              y_ref, convo_ref, ssmo_ref,
              xp_ref, st_ref, act_ref, dte_ref, acse_ref, *, lb, valid, nc):
    c = pl.program_id(1)
    q = SSD_CHUNK

    @pl.when(c == 0)
    def _():
        if lb < q:
            xp_ref[...] = jnp.zeros_like(xp_ref)
        xp_ref[XP_BASE - 3:XP_BASE, :] = conv0_ref[...]
        for g in range(SSD_N_GROUPS):
            st_ref[g] = ssm0_ref[g * SSD_GROUP_W:(g + 1) * SSD_GROUP_W, :].T

    xp_ref[XP_BASE:XP_BASE + lb, :] = xbc_ref[...]

    cw = 512
    for j in range(SSD_CONV_DIM // cw):
        sl = slice(j * cw, (j + 1) * cw)
        conv = cb_ref[:, sl]
        for t in range(SSD_CONV_W):
            conv = conv + cw_ref[t:t + 1, sl] * xp_ref[XP_BASE - 3 + t:XP_BASE - 3 + t + q, sl]
        act_ref[:, sl] = _silu(conv)

    @pl.when(c == nc - 1)
    def _():
        convo_ref[...] = xp_ref[XP_BASE + valid - 3:XP_BASE + valid, :]

    xp_ref[XP_BASE - 3:XP_BASE, :] = xp_ref[XP_BASE + q - 3:XP_BASE + q, :]

    sm = sm_ref[...]
    if lb < q:
        sm = jnp.concatenate([sm, jnp.zeros((q - lb, LANES), F32)], axis=0)
    row = lax.broadcasted_iota(I32, (q, LANES), 0)
    lane = lax.broadcasted_iota(I32, (q, LANES), 1)
    xdt = sm + dtb_ref[...]
    dt = jnp.maximum(xdt, 0.0) + jnp.log1p(jnp.exp(-jnp.abs(xdt)))
    dt = jnp.where((lane < SSD_N_HEADS) & (row < valid), dt, 0.0)
    a = -jnp.exp(alog_ref[...])
    da = jnp.where(lane < SSD_N_HEADS, dt * a, 0.0)
    tri = tri_ref[...]
    acs = sum(_dot(tri, p) for p in _split_bf16(da, 3))
    acs_t = acs.T
    expand = exp_ref[...]
    dte_ref[...] = sum(_dot(p, expand) for p in _split_bf16(dt, 3))
    acse_ref[...] = sum(_dot(p, expand) for p in _split_bf16(acs, 3))

    tril = row >= lane
    for g in range(SSD_N_GROUPS):
        sl = slice(g * SSD_GROUP_W, (g + 1) * SSD_GROUP_W)
        xs_g = act_ref[:, sl]
        bm_g = act_ref[:, SSD_D_INNER + g * SSD_D_STATE:SSD_D_INNER + (g + 1) * SSD_D_STATE]
        cm_g = act_ref[:, SSD_D_INNER + SSD_N_GROUPS * SSD_D_STATE + g * SSD_D_STATE:
                       SSD_D_INNER + SSD_N_GROUPS * SSD_D_STATE + (g + 1) * SSD_D_STATE]
        acse_g = acse_ref[:, sl]
        alast_g = acse_ref[q - 1:q, sl]
        xdt_g = xs_g * dte_ref[:, sl]
        xw_g = xdt_g * jnp.exp(alast_g - acse_g)
        cmb = cm_g.astype(BF16)
        bmb = bm_g.astype(BF16)
        cb = _dot_nt(cmb, bmb)
        ys = []
        for r in range(SSD_HEADS_PER_GROUP):
            h = g * SSD_HEADS_PER_GROUP + r
            col = jnp.broadcast_to(acs[:, h:h + 1], (q, q))
            rowv = acs_t[h:h + 1, :]
            lmat = jnp.where(tril, jnp.exp(col - rowv), 0.0) * cb
            ys.append(_dot(lmat.astype(BF16),
                           xdt_g[:, r * SSD_HEAD_DIM:(r + 1) * SSD_HEAD_DIM].astype(BF16)))
        y = jnp.concatenate(ys, axis=1)
        st = st_ref[g]
        y = y + _dot(cmb, st.astype(BF16)) * jnp.exp(acse_g)
        st_ref[g] = st * jnp.exp(alast_g) + _dot(bm_g.T.astype(BF16), xw_g.astype(BF16))
        y = y + dsk_ref[:, sl] * xs_g
        yz = y[:lb] * _silu(z_ref[:, sl])
        y_ref[:, sl] = yz * lax.rsqrt(jnp.mean(yz * yz, axis=-1, keepdims=True) + EPS) * ng_ref[:, sl]

    @pl.when(c == nc - 1)
    def _():
        for g in range(SSD_N_GROUPS):
            ssmo_ref[g * SSD_GROUP_W:(g + 1) * SSD_GROUP_W, :] = st_ref[g].T


def _ssd(p3, conv0, ssm0, cw, cb, dtb, alog, dsk, ng, tri, expand, *, lb, valid):
    b, l, _ = p3.shape
    nc = l // lb
    q = SSD_CHUNK
    const = lambda shape: pl.BlockSpec(shape, lambda i, c: (0,) * len(shape))
    return pl.pallas_call(
        functools.partial(_ssd_body, lb=lb, valid=valid, nc=nc),
        grid=(b, nc),
        in_specs=[
            pl.BlockSpec((None, lb, SSD_CONV_DIM), lambda i, c: (i, c, P_XBC // SSD_CONV_DIM)),
            pl.BlockSpec((None, lb, SSD_D_INNER), lambda i, c: (i, c, P_Z // SSD_D_INNER)),
            pl.BlockSpec((None, lb, LANES), lambda i, c: (i, c, P_SMALL // LANES)),
            pl.BlockSpec((None, SSD_CONV_W - 1, SSD_CONV_DIM), lambda i, c: (i, 0, 0)),
            pl.BlockSpec((None, SSD_D_INNER, SSD_D_STATE), lambda i, c: (i, 0, 0)),
            const((SSD_CONV_W, SSD_CONV_DIM)),
            const((1, SSD_CONV_DIM)),
            const((1, LANES)),
            const((1, LANES)),
            const((1, SSD_D_INNER)),
            const((1, SSD_D_INNER)),
            const((q, q)),
            const((LANES, SSD_D_INNER)),
        ],
        out_specs=[
            pl.BlockSpec((None, lb, SSD_D_INNER), lambda i, c: (i, c, 0)),
            pl.BlockSpec((None, SSD_CONV_W - 1, SSD_CONV_DIM), lambda i, c: (i, 0, 0)),
            pl.BlockSpec((None, SSD_D_INNER, SSD_D_STATE), lambda i, c: (i, 0, 0)),
        ],
        out_shape=[
            jax.ShapeDtypeStruct((b, l, SSD_D_INNER), F32),
            jax.ShapeDtypeStruct((b, SSD_CONV_W - 1, SSD_CONV_DIM), F32),
            jax.ShapeDtypeStruct((b, SSD_D_INNER, SSD_D_STATE), F32),
        ],
        scratch_shapes=[
            pltpu.VMEM((XP_BASE + q, SSD_CONV_DIM), F32),
            pltpu.VMEM((SSD_N_GROUPS, SSD_D_STATE, SSD_GROUP_W), F32),
            pltpu.VMEM((q, SSD_CONV_DIM), F32),
            pltpu.VMEM((q, SSD_D_INNER), F32),
            pltpu.VMEM((q, SSD_D_INNER), F32),
        ],
        compiler_params=_cparams(("parallel", "arbitrary")),
        name="ssd",
    )(p3, p3, p3, conv0, ssm0, cw, cb, dtb, alog, dsk, ng, tri, expand)


def _count(pred):
    return jnp.sum(jnp.where(pred, 1.0, 0.0), axis=-1, keepdims=True)


def _topk_mask(score, topk, idx_bits):
    n = score.shape[-1]
    score = score + 0.0
    bits = lax.bitcast_convert_type(score, I32)
    key = jnp.where(bits < 0, bits ^ jnp.int32(0x7FFFFFFF), bits)
    kf = float(topk)

    thr = jnp.where(_count(key >= 0) >= kf, jnp.int32(0), jnp.int32(INT_MIN))

    def vbody(i, t):
        cand = t + (jnp.int32(1) << (30 - i))
        return jnp.where(_count(key >= cand) >= kf, cand, t)

    thr = lax.fori_loop(0, 31, vbody, thr)
    gt = key > thr
    eq = key == thr
    need = kf - _count(gt)
    idx = lax.broadcasted_iota(I32, score.shape, 1)

    def ibody(i, t):
        cand = t + (jnp.int32(1) << (idx_bits - 1 - i))
        return jnp.where(_count(eq & (idx < cand)) < need, cand, t)

    cut = lax.fori_loop(0, idx_bits, ibody, jnp.zeros((score.shape[0], 1), I32))
    return gt | (eq & (idx <= cut))


IDX_DOT_SCALE = IDX_HEAD_DIM ** -0.5
IDX_W_SCALE = IDX_N_HEADS ** -0.5
ATT_SCALE = ATT_HEAD_DIM ** -0.5


def _dsa_prompt_body(q_ref, qi_ref, sm_ref, k_ref, v_ref, sma_ref, o_ref, kb_ref, vb_ref, kib_ref,
                     *, topk, idx_bits):
    qb = pl.program_id(1)
    n = k_ref.shape[0]

    @pl.when(qb == 0)
    def _():
        kb_ref[...] = k_ref[...].astype(BF16)
        vb_ref[...] = v_ref[...].astype(BF16)
        kib_ref[...] = sma_ref[:, SM_KI:SM_KI + IDX_HEAD_DIM].astype(BF16)

    qi = qi_ref[...]
    wi = sm_ref[:, SM_WI:SM_WI + IDX_N_HEADS]
    kib = kib_ref[...]
    score = jnp.zeros((Q_BLOCK, n), F32)
    for h in range(IDX_N_HEADS):
        d = _dot_nt(qi[:, h * IDX_HEAD_DIM:(h + 1) * IDX_HEAD_DIM].astype(BF16), kib)
        score = score + (wi[:, h:h + 1] * IDX_W_SCALE) * jnp.maximum(d * IDX_DOT_SCALE, 0.0)
    qpos = qb * Q_BLOCK + lax.broadcasted_iota(I32, (Q_BLOCK, n), 0)
    kpos = lax.broadcasted_iota(I32, (Q_BLOCK, n), 1)
    allowed = kpos <= qpos
    score = jnp.where(allowed, score, -jnp.inf)
    mask = _topk_mask(score, topk, idx_bits) & allowed

    q = q_ref[...]
    outs = []
    for g in range(ATT_N_KV_HEADS):
        kg = kb_ref[:, g * ATT_HEAD_DIM:(g + 1) * ATT_HEAD_DIM]
        vg = vb_ref[:, g * ATT_HEAD_DIM:(g + 1) * ATT_HEAD_DIM]
        for r in range(ATT_REP):
            hh = g * ATT_REP + r
            s = _dot_nt(q[:, hh * ATT_HEAD_DIM:(hh + 1) * ATT_HEAD_DIM].astype(BF16), kg) * ATT_SCALE
            s = jnp.where(mask, s, -jnp.inf)
            p = jnp.exp(s - jnp.max(s, axis=-1, keepdims=True))
            den = jnp.sum(p, axis=-1, keepdims=True)
            outs.append(_dot(p.astype(BF16), vg) / den)
    o_ref[...] = jnp.concatenate(outs, axis=1)


def _dsa_prompt(p3):
    b, l, _ = p3.shape
    topk = min(TOPK_MAX, l // 4)
    idx_bits = max(1, (l - 1).bit_length())
    aw = ATT_N_HEADS * ATT_HEAD_DIM
    qiw = IDX_N_HEADS * IDX_HEAD_DIM
    return pl.pallas_call(
        functools.partial(_dsa_prompt_body, topk=topk, idx_bits=idx_bits),
        grid=(b, l // Q_BLOCK),
        in_specs=[
            pl.BlockSpec((None, Q_BLOCK, aw), lambda i, j: (i, j, P_Q // aw)),
            pl.BlockSpec((None, Q_BLOCK, qiw), lambda i, j: (i, j, P_QI // qiw)),
            pl.BlockSpec((None, Q_BLOCK, LANES), lambda i, j: (i, j, P_SMALL // LANES)),
            pl.BlockSpec((None, l, ATT_KV_W), lambda i, j: (i, 0, P_K // ATT_KV_W)),
            pl.BlockSpec((None, l, ATT_KV_W), lambda i, j: (i, 0, P_V // ATT_KV_W)),
            pl.BlockSpec((None, l, LANES), lambda i, j: (i, 0, P_SMALL // LANES)),
        ],
        out_specs=pl.BlockSpec((None, Q_BLOCK, aw), lambda i, j: (i, j, 0)),
        out_shape=jax.ShapeDtypeStruct((b, l, aw), F32),
        scratch_shapes=[
            pltpu.VMEM((l, ATT_KV_W), BF16),
            pltpu.VMEM((l, ATT_KV_W), BF16),
            pltpu.VMEM((l, IDX_HEAD_DIM), BF16),
        ],
        compiler_params=_cparams(("parallel", "arbitrary")),
        name="dsa_prompt",
    )(p3, p3, p3, p3, p3, p3)


SCORE_KEYS = 2048


def _dsa_sample_score_body(pt_ref, qi_ref, sm_ref, kidx_hbm, o_ref, kbuf, sem, *, n_pages, valid):
    b = pl.program_id(0)
    nb = pl.num_programs(0)
    past = n_pages * PAGE_SIZE

    def page_copy(bb, pg, slot):
        return pltpu.make_async_copy(kidx_hbm.at[pt_ref[bb, pg]],
                                     kbuf.at[slot, pl.ds(pg * PAGE_SIZE, PAGE_SIZE), :],
                                     sem.at[slot])

    def fetch(bb, slot):
        def body(pg, carry):
            page_copy(bb, pg, slot).start()
            return carry
        lax.fori_loop(0, n_pages, body, 0)

    @pl.when(b == 0)
    def _():
        fetch(0, 0)

    @pl.when(b + 1 < nb)
    def _():
        fetch(b + 1, (b + 1) % 2)

    slot = b % 2

    def wbody(pg, carry):
        page_copy(b, pg, slot).wait()
        return carry
    lax.fori_loop(0, n_pages, wbody, 0)

    qi = qi_ref[...]
    qh = jnp.concatenate([qi[:, h * IDX_HEAD_DIM:(h + 1) * IDX_HEAD_DIM] for h in range(IDX_N_HEADS)],
                         axis=0).astype(BF16)
    wi = sm_ref[:, SM_WI:SM_WI + IDX_N_HEADS]
    wcol = jnp.concatenate([wi[:, h:h + 1] for h in range(IDX_N_HEADS)], axis=0) * IDX_W_SCALE

    def head_sum(d):
        w = wcol * jnp.maximum(d * IDX_DOT_SCALE, 0.0)
        sc = w[0:SAMPLE_PAD]
        for h in range(1, IDX_N_HEADS):
            sc = sc + w[h * SAMPLE_PAD:(h + 1) * SAMPLE_PAD]
        return sc

    for kc in range(past // SCORE_KEYS):
        kk = kbuf[slot, kc * SCORE_KEYS:(kc + 1) * SCORE_KEYS, :].astype(BF16)
        sc = head_sum(_dot_nt(qh, kk))
        o_ref[:, kc * SCORE_KEYS:(kc + 1) * SCORE_KEYS] = sc[0:SAMPLE_ROWS]

    kin = sm_ref[:, SM_KI:SM_KI + IDX_HEAD_DIM]
    kin = jnp.concatenate([kin, jnp.zeros((LANES - SAMPLE_PAD, IDX_HEAD_DIM), F32)], axis=0).astype(BF16)
    sc = head_sum(_dot_nt(qh, kin))
    trow = lax.broadcasted_iota(I32, (SAMPLE_PAD, LANES), 0)
    tcol = lax.broadcasted_iota(I32, (SAMPLE_PAD, LANES), 1)
    sc = jnp.where((tcol <= trow) & (tcol < valid), sc, -jnp.inf)
    o_ref[:, past:past + LANES] = sc[0:SAMPLE_ROWS]


def _dsa_sample_score(p3, kidx_pool, page_table, *, valid):
    b = p3.shape[0]
    n_pages = page_table.shape[1]
    past = n_pages * PAGE_SIZE
    qiw = IDX_N_HEADS * IDX_HEAD_DIM
    grid_spec = pltpu.PrefetchScalarGridSpec(
        num_scalar_prefetch=1,
        grid=(b,),
        in_specs=[
            pl.BlockSpec((None, SAMPLE_PAD, qiw), lambda i, pt: (i, 0, P_QI // qiw)),
            pl.BlockSpec((None, SAMPLE_PAD, LANES), lambda i, pt: (i, 0, P_SMALL // LANES)),
            pl.BlockSpec(memory_space=pl.ANY),
        ],
        out_specs=pl.BlockSpec((None, SAMPLE_ROWS, past + LANES), lambda i, pt: (i, 0, 0)),
        scratch_shapes=[
            pltpu.VMEM((2, past, IDX_HEAD_DIM), F32),
            pltpu.SemaphoreType.DMA((2,)),
        ],
    )
    return pl.pallas_call(
        functools.partial(_dsa_sample_score_body, n_pages=n_pages, valid=valid),
        grid_spec=grid_spec,
        out_shape=jax.ShapeDtypeStruct((b, SAMPLE_ROWS, past + LANES), F32),
        compiler_params=_cparams(("arbitrary",)),
        name="dsa_sample_score",
    )(page_table, p3, p3, kidx_pool)


def _select_body(s_ref, o_ref, *, topk, idx_bits):
    score = s_ref[...]
    mask = _topk_mask(score, topk, idx_bits) & (score > -jnp.inf)
    o_ref[...] = jnp.where(mask, 1.0, 0.0)


def _select(score2d, *, topk, tr):
    r, n = score2d.shape
    idx_bits = max(1, (n - 1).bit_length())
    return pl.pallas_call(
        functools.partial(_select_body, topk=topk, idx_bits=idx_bits),
        grid=(r // tr,),
        in_specs=[pl.BlockSpec((tr, n), lambda i: (i, 0))],
        out_specs=pl.BlockSpec((tr, n), lambda i: (i, 0)),
        out_shape=jax.ShapeDtypeStruct((r, n), F32),
        compiler_params=_cparams(("parallel",)),
        name="dsa_sample_select",
    )(score2d)


ATT_ROWS = ATT_N_HEADS * SAMPLE_ROWS


def _dsa_sample_attn_body(pt_ref, q_ref, kn_ref, vn_ref, mask_ref, maskn_ref, kpool, vpool, o_ref,
                          kbuf, vbuf, sem, qbd_ref, m_ref, l_ref, acc_ref, *, n_chunks, ppc):
    b = pl.program_id(0)
    c = pl.program_id(1)
    nb = pl.num_programs(0)
    step = b * n_chunks + c
    ck = ppc * PAGE_SIZE

    def page_copies(bb, cc, j, slot):
        pg = pt_ref[bb, cc * ppc + j]
        dst = pl.ds(j * PAGE_SIZE, PAGE_SIZE)
        return (pltpu.make_async_copy(kpool.at[pg], kbuf.at[slot, dst, :], sem.at[0, slot]),
                pltpu.make_async_copy(vpool.at[pg], vbuf.at[slot, dst, :], sem.at[1, slot]))

    def fetch(st, slot):
        bb = st // n_chunks
        cc = st % n_chunks
        for j in range(ppc):
            for cp in page_copies(bb, cc, j, slot):
                cp.start()

    @pl.when(step == 0)
    def _():
        fetch(0, 0)

    @pl.when(step + 1 < nb * n_chunks)
    def _():
        fetch(step + 1, (step + 1) % 2)

    slot = step % 2
    for j in range(ppc):
        for cp in page_copies(b, c, j, slot):
            cp.wait()

    @pl.when(c == 0)
    def _():
        q = q_ref[...]
        qbd_ref[...] = jnp.zeros_like(qbd_ref)
        for g in range(ATT_N_KV_HEADS):
            for r in range(ATT_REP):
                hh = g * ATT_REP + r
                r0 = (g * ATT_REP + r) * SAMPLE_ROWS
                qbd_ref[r0:r0 + SAMPLE_ROWS, g * ATT_HEAD_DIM:(g + 1) * ATT_HEAD_DIM] = (
                    q[0:SAMPLE_ROWS, hh * ATT_HEAD_DIM:(hh + 1) * ATT_HEAD_DIM])
        m_ref[...] = jnp.full_like(m_ref, NEG_BIG)
        l_ref[...] = jnp.zeros_like(l_ref)
        acc_ref[...] = jnp.zeros_like(acc_ref)

    qbd = qbd_ref[...].astype(BF16)

    def update(kmat, vmat, msk8):
        s = _dot_nt(qbd, kmat) * ATT_SCALE
        mk = jnp.tile(msk8, (ATT_N_HEADS, 1)) > 0.0
        m_old = m_ref[...]
        m_new = jnp.maximum(m_old, jnp.max(jnp.where(mk, s, NEG_BIG), axis=-1, keepdims=True))
        p = jnp.where(mk, jnp.exp(s - m_new), 0.0)
        alpha = jnp.exp(m_old - m_new)
        l_ref[...] = alpha * l_ref[...] + jnp.sum(p, axis=-1, keepdims=True)
        acc_ref[...] = alpha * acc_ref[...] + _dot(p.astype(BF16), vmat)
        m_ref[...] = m_new

    update(kbuf[slot].astype(BF16), vbuf[slot].astype(BF16), mask_ref[...])

    @pl.when(c == n_chunks - 1)
    def _():
        pad = jnp.zeros((LANES - SAMPLE_PAD, ATT_KV_W), F32)
        kn = jnp.concatenate([kn_ref[...], pad], axis=0).astype(BF16)
        vn = jnp.concatenate([vn_ref[...], pad], axis=0).astype(BF16)
        update(kn, vn, maskn_ref[...])
        o = acc_ref[...] / l_ref[...]
        o_ref[...] = jnp.zeros_like(o_ref)
        for g in range(ATT_N_KV_HEADS):
            for r in range(ATT_REP):
                hh = g * ATT_REP + r
                r0 = (g * ATT_REP + r) * SAMPLE_ROWS
                o_ref[0:SAMPLE_ROWS, hh * ATT_HEAD_DIM:(hh + 1) * ATT_HEAD_DIM] = (
                    o[r0:r0 + SAMPLE_ROWS, g * ATT_HEAD_DIM:(g + 1) * ATT_HEAD_DIM])


def _dsa_sample_attn(p3, mask3, kpool, vpool, page_table, *, ppc=16):
    b = p3.shape[0]
    n_pages = page_table.shape[1]
    n_chunks = n_pages // ppc
    ck = ppc * PAGE_SIZE
    past = n_pages * PAGE_SIZE
    aw = ATT_N_HEADS * ATT_HEAD_DIM
    grid_spec = pltpu.PrefetchScalarGridSpec(
        num_scalar_prefetch=1,
        grid=(b, n_chunks),
        in_specs=[
            pl.BlockSpec((None, SAMPLE_PAD, aw), lambda i, c, pt: (i, 0, P_Q // aw)),
            pl.BlockSpec((None, SAMPLE_PAD, ATT_KV_W), lambda i, c, pt: (i, 0, P_K // ATT_KV_W)),
            pl.BlockSpec((None, SAMPLE_PAD, ATT_KV_W), lambda i, c, pt: (i, 0, P_V // ATT_KV_W)),
            pl.BlockSpec((None, SAMPLE_ROWS, ck), lambda i, c, pt: (i, 0, c)),
            pl.BlockSpec((None, SAMPLE_ROWS, LANES), lambda i, c, pt: (i, 0, past // LANES)),
            pl.BlockSpec(memory_space=pl.ANY),
            pl.BlockSpec(memory_space=pl.ANY),
        ],
        out_specs=pl.BlockSpec((None, SAMPLE_PAD, aw), lambda i, c, pt: (i, 0, 0)),
        scratch_shapes=[
            pltpu.VMEM((2, ck, ATT_KV_W), F32),
            pltpu.VMEM((2, ck, ATT_KV_W), F32),
            pltpu.SemaphoreType.DMA((2, 2)),
            pltpu.VMEM((ATT_ROWS, ATT_KV_W), F32),
            pltpu.VMEM((ATT_ROWS, 1), F32),
            pltpu.VMEM((ATT_ROWS, 1), F32),
            pltpu.VMEM((ATT_ROWS, ATT_KV_W), F32),
        ],
    )
    return pl.pallas_call(
        functools.partial(_dsa_sample_attn_body, n_chunks=n_chunks, ppc=ppc),
        grid_spec=grid_spec,
        out_shape=jax.ShapeDtypeStruct((b, SAMPLE_PAD, aw), F32),
        compiler_params=_cparams(("arbitrary", "arbitrary")),
        name="dsa_sample_attn",
    )(page_table, p3, p3, p3, mask3, mask3, kpool, vpool)


def _merge_body(h_ref, ys_ref, ya_ref, gs_ref, ga_ref, wbs_ref, wba_ref, wo_ref, pg_ref, o_ref):
    a = _dot(ys_ref[...].astype(BF16), wbs_ref[...])
    bb = _dot(ya_ref[...].astype(BF16), wba_ref[...])
    merged = _sigmoid(gs_ref[...]) * a + _sigmoid(ga_ref[...]) * bb
    o_ref[...] = h_ref[...] + _rms(_dot(merged.astype(BF16), wo_ref[...]), pg_ref[...])


def _merge(h, y_ssd, y_att, p2, wbs, wba, wo, post_g, *, tm):
    m = h.shape[0]
    const = lambda shape: pl.BlockSpec(shape, lambda i: (0,) * len(shape))
    return pl.pallas_call(
        _merge_body,
        grid=(m // tm,),
        in_specs=[
            pl.BlockSpec((tm, D_MODEL), lambda i: (i, 0)),
            pl.BlockSpec((tm, SSD_D_INNER), lambda i: (i, 0)),
            pl.BlockSpec((tm, D_MODEL), lambda i: (i, 0)),
            pl.BlockSpec((tm, D_MODEL), lambda i: (i, P_GSSD // D_MODEL)),
            pl.BlockSpec((tm, D_MODEL), lambda i: (i, P_GATT // D_MODEL)),
            const((SSD_D_INNER, D_MODEL)),
            const((D_MODEL, D_MODEL)),
            const((D_MODEL, D_MODEL)),
            const((1, D_MODEL)),
        ],
        out_specs=pl.BlockSpec((tm, D_MODEL), lambda i: (i, 0)),
        out_shape=jax.ShapeDtypeStruct((m, D_MODEL), F32),
        compiler_params=_cparams(("parallel",)),
        name="merge",
    )(h, y_ssd, y_att, p2, p2, wbs, wba, wo, post_g)


MEM_SCALE = MEM_HEAD_DIM ** -0.5


def _mem_body(h_ref, mk_ref, mv_ref, pg_ref, wq_ref, wo_ref, qg_ref, o_ref):
    h = h_ref[...]
    q = _dot(_rms(h, pg_ref[...]).astype(BF16), wq_ref[...])
    outs = []
    for hd in range(MEM_N_HEADS):
        sl = slice(hd * MEM_HEAD_DIM, (hd + 1) * MEM_HEAD_DIM)
        s = _dot_nt(q[:, sl].astype(BF16), mk_ref[:, sl].astype(BF16)) * MEM_SCALE
        p = jnp.exp(s - jnp.max(s, axis=-1, keepdims=True))
        den = jnp.sum(p, axis=-1, keepdims=True)
        outs.append(_dot(p.astype(BF16), mv_ref[:, sl].astype(BF16)) / den)
    att = jnp.concatenate(outs, axis=1)
    o_ref[...] = h + _rms(_dot(att.astype(BF16), wo_ref[...]), qg_ref[...])


def _mem_attn(h3, mk, mv, pre_g, wq, wo, post_g, *, tl):
    b, l, _ = h3.shape
    const = lambda shape: pl.BlockSpec(shape, lambda i, j: (0,) * len(shape))
    return pl.pallas_call(
        _mem_body,
        grid=(b, l // tl),
        in_specs=[
            pl.BlockSpec((None, tl, D_MODEL), lambda i, j: (i, j, 0)),
            pl.BlockSpec((None, MEM_LEN, D_MODEL), lambda i, j: (i, 0, 0)),
            pl.BlockSpec((None, MEM_LEN, D_MODEL), lambda i, j: (i, 0, 0)),
            const((1, D_MODEL)),
            const((D_MODEL, D_MODEL)),
            const((D_MODEL, D_MODEL)),
            const((1, D_MODEL)),
        ],
        out_specs=pl.BlockSpec((None, tl, D_MODEL), lambda i, j: (i, j, 0)),
        out_shape=jax.ShapeDtypeStruct((b, l, D_MODEL), F32),
        compiler_params=_cparams(("parallel", "parallel")),
        name="mem_attn",
    )(h3, mk, mv, pre_g, wq, wo, post_g)


def _prep_weights(w):
    wi = w["w_in"]
    offs = {}
    o = 0
    for name, width in (("z", SSD_D_INNER), ("xbc", SSD_CONV_DIM), ("dt", SSD_N_HEADS),
                        ("q", ATT_N_HEADS * ATT_HEAD_DIM), ("k", ATT_KV_W), ("v", ATT_KV_W),
                        ("qi", IDX_N_HEADS * IDX_HEAD_DIM), ("ki", IDX_HEAD_DIM), ("wi", IDX_N_HEADS),
                        ("gs", D_MODEL), ("ga", D_MODEL)):
        offs[name] = (o, o + width)
        o += width
    seg = lambda n: wi[:, offs[n][0]:offs[n][1]]
    pad = jnp.zeros((D_MODEL, P_WIDTH - P_SMALL - SSD_N_HEADS - IDX_HEAD_DIM - IDX_N_HEADS), wi.dtype)
    w_in = jnp.concatenate([seg("xbc"), seg("z"), seg("q"), seg("k"), seg("v"), seg("qi"), seg("gs"),
                            seg("ga"), seg("dt"), seg("ki"), seg("wi"), pad], axis=1).astype(BF16)
    lane_pad = lambda v: jnp.pad(v, (0, LANES - v.shape[0]))[None, :]
    head = jnp.arange(SSD_D_INNER) // SSD_HEAD_DIM
    out = dict(w)
    out.update(
        w_in=w_in,
        dt_bias=lane_pad(w["dt_bias"]),
        a_log=lane_pad(w["a_log"]),
        d_skip=w["d_skip"][head][None, :],
        tri=(jnp.arange(SSD_CHUNK)[:, None] >= jnp.arange(SSD_CHUNK)[None, :]).astype(BF16),
        expand=(jnp.arange(LANES)[:, None] == head[None, :]).astype(BF16),
    )
    for name in ("ffn1_wg", "ffn1_wu", "ffn1_wd", "ffn2_wg", "ffn2_wu", "ffn2_wd", "w_br_ssd", "w_br_att",
                 "w_out", "w_mq", "w_mk", "w_mv", "w_mo"):
        out[name] = w[name].astype(BF16)
    for name in ("ffn1_pre_g", "ffn1_post_g", "mix_pre_g", "mix_post_g", "mem_pre_g", "mem_kv_g",
                 "mem_post_g", "ffn2_pre_g", "ffn2_post_g", "conv_b", "ssd_norm_g"):
        out[name] = w[name][None, :]
    return out


def _layer(x3, w, conv0, ssm0, mem_k, mem_v, attn_fn, *, tm, tn, lb, valid, tl):
    b, l, _ = x3.shape
    x = x3.reshape(b * l, D_MODEL)
    h = _ffn(x, w["ffn1_pre_g"], w["ffn1_wg"], w["ffn1_wu"], w["ffn1_wd"], w["ffn1_post_g"], tm=tm)
    p2 = _norm_proj(h, w["mix_pre_g"], w["w_in"], tm=tm, tn=tn)
    p3 = p2.reshape(b, l, P_WIDTH)
    y_ssd, conv_new, ssm_new = _ssd(p3, conv0, ssm0, w["conv_w"], w["conv_b"], w["dt_bias"], w["a_log"],
                                    w["d_skip"], w["ssd_norm_g"], w["tri"], w["expand"], lb=lb, valid=valid)
    y_att = attn_fn(p3)
    h = _merge(h, y_ssd.reshape(b * l, SSD_D_INNER), y_att.reshape(b * l, D_MODEL), p2,
               w["w_br_ssd"], w["w_br_att"], w["w_out"], w["mix_post_g"], tm=min(tm, 512))
    h = _mem_attn(h.reshape(b, l, D_MODEL), mem_k, mem_v, w["mem_pre_g"], w["w_mq"], w["w_mo"],
                  w["mem_post_g"], tl=tl).reshape(b * l, D_MODEL)
    h = _ffn(h, w["ffn2_pre_g"], w["ffn2_wg"], w["ffn2_wu"], w["ffn2_wd"], w["ffn2_post_g"], tm=tm)
    return h.reshape(b, l, D_MODEL), p3, conv_new, ssm_new


def _kv_rows(p3, n):
    b = p3.shape[0]
    k = p3[:, :n, P_K:P_K + ATT_KV_W].reshape(1, b, n, ATT_N_KV_HEADS, ATT_HEAD_DIM)
    v = p3[:, :n, P_V:P_V + ATT_KV_W].reshape(1, b, n, ATT_N_KV_HEADS, ATT_HEAD_DIM)
    ki = p3[:, :n, P_SMALL + SM_KI:P_SMALL + SM_KI + IDX_HEAD_DIM][None]
    return k, v, ki


def kernel(x_prompt, x_sample, mem_prompt, cache_k, cache_v, cache_kidx, state_ssm, state_conv, cache_mem_k, cache_mem_v, page_table, ffn1_pre_g, ffn1_wg, ffn1_wu, ffn1_wd, ffn1_post_g, mix_pre_g, w_in, conv_w, conv_b, dt_bias, a_log, d_skip, ssd_norm_g, w_br_ssd, w_br_att, w_out, mix_post_g, mem_pre_g, mem_kv_g, w_mq, w_mk, w_mv, w_mo, mem_post_g, ffn2_pre_g, ffn2_wg, ffn2_wu, ffn2_wd, ffn2_post_g):
    assert ffn1_wg.shape[0] == 1, "one trunk layer"
    w = _prep_weights(dict(
        ffn1_pre_g=ffn1_pre_g[0], ffn1_wg=ffn1_wg[0], ffn1_wu=ffn1_wu[0], ffn1_wd=ffn1_wd[0],
        ffn1_post_g=ffn1_post_g[0], mix_pre_g=mix_pre_g[0], w_in=w_in[0], conv_w=conv_w[0], conv_b=conv_b[0],
        dt_bias=dt_bias[0], a_log=a_log[0], d_skip=d_skip[0], ssd_norm_g=ssd_norm_g[0], w_br_ssd=w_br_ssd[0],
        w_br_att=w_br_att[0], w_out=w_out[0], mix_post_g=mix_post_g[0], mem_pre_g=mem_pre_g[0],
        mem_kv_g=mem_kv_g[0], w_mq=w_mq[0], w_mk=w_mk[0], w_mv=w_mv[0], w_mo=w_mo[0],
        mem_post_g=mem_post_g[0], ffn2_pre_g=ffn2_pre_g[0], ffn2_wg=ffn2_wg[0], ffn2_wu=ffn2_wu[0],
        ffn2_wd=ffn2_wd[0], ffn2_post_g=ffn2_post_g[0]))

    bp, lp, _ = x_prompt.shape
    mem2 = mem_prompt.reshape(bp * MEM_LEN, D_MODEL)
    mk_p = _norm_proj(mem2, w["mem_kv_g"], w["w_mk"], tm=512, tn=D_MODEL).reshape(bp, MEM_LEN, D_MODEL)
    mv_p = _norm_proj(mem2, w["mem_kv_g"], w["w_mv"], tm=512, tn=D_MODEL).reshape(bp, MEM_LEN, D_MODEL)
    conv0 = jnp.zeros((bp, SSD_CONV_W - 1, SSD_CONV_DIM), F32)
    ssm0 = jnp.zeros((bp, SSD_D_INNER, SSD_D_STATE), F32)
    y_p, p3_p, conv_p, ssm_p = _layer(x_prompt, w, conv0, ssm0, mk_p, mv_p, _dsa_prompt,
                                      tm=1024, tn=1152, lb=SSD_CHUNK, valid=SSD_CHUNK, tl=512)
    k_p, v_p, ki_p = _kv_rows(p3_p, lp)

    bs, ls, _ = x_sample.shape
    assert ls <= SAMPLE_ROWS
    n_pages = page_table.shape[1]
    past = n_pages * PAGE_SIZE
    n_pool = cache_k.shape[1]
    kpool = cache_k[0].reshape(n_pool, PAGE_SIZE, ATT_KV_W)
    vpool = cache_v[0].reshape(n_pool, PAGE_SIZE, ATT_KV_W)
    topk_s = min(TOPK_MAX, (past + ls) // 4)

    def sample_attn(p3):
        score = _dsa_sample_score(p3, cache_kidx[0], page_table, valid=ls)
        mask = _select(score.reshape(bs * SAMPLE_ROWS, past + LANES), topk=topk_s, tr=32)
        return _dsa_sample_attn(p3, mask.reshape(bs, SAMPLE_ROWS, past + LANES), kpool, vpool, page_table)

    xs = jnp.pad(x_sample, ((0, 0), (0, SAMPLE_PAD - ls), (0, 0)))
    y_s, p3_s, conv_s, ssm_s = _layer(
        xs, w, state_conv[0], state_ssm[0].reshape(bs, SSD_D_INNER, SSD_D_STATE),
        cache_mem_k[0].reshape(bs, MEM_LEN, D_MODEL), cache_mem_v[0].reshape(bs, MEM_LEN, D_MODEL),
        sample_attn, tm=bs * SAMPLE_PAD, tn=1152, lb=SAMPLE_PAD, valid=ls, tl=SAMPLE_PAD)
    k_s, v_s, ki_s = _kv_rows(p3_s, ls)

    ssm_shape = (1, -1, SSD_N_HEADS, SSD_HEAD_DIM, SSD_D_STATE)
    mem_shape = (1, bp, MEM_LEN, MEM_N_HEADS, MEM_HEAD_DIM)
    return (y_p, y_s[:, :ls], k_p, v_p, ki_p, ssm_p.reshape(ssm_shape), conv_p[None],
            mk_p.reshape(mem_shape), mv_p.reshape(mem_shape),
            k_s, v_s, ki_s, ssm_s.reshape(ssm_shape), conv_s[None])
```

```python
import functools
import math

import jax
import jax.numpy as jnp
from jax import lax
from jax.experimental import pallas as pl
from jax.experimental.pallas import tpu as pltpu

F32 = jnp.float32
BF16 = jnp.bfloat16
I32 = jnp.int32

D_MODEL = 1024
EPS = 1e-6
FFN_HIDDEN = 2816
SSD_D_INNER = 2048
SSD_HEAD_DIM = 64
SSD_N_HEADS = 32
SSD_N_GROUPS = 8
SSD_D_STATE = 128
SSD_CONV_W = 4
SSD_CONV_DIM = 4096
SSD_CHUNK = 128
SSD_GROUP_W = SSD_D_INNER // SSD_N_GROUPS
SSD_HEADS_PER_GROUP = SSD_N_HEADS // SSD_N_GROUPS
ATT_N_HEADS = 16
ATT_N_KV_HEADS = 4
ATT_HEAD_DIM = 64
ATT_REP = ATT_N_HEADS // ATT_N_KV_HEADS
ATT_KV_W = ATT_N_KV_HEADS * ATT_HEAD_DIM
IDX_N_HEADS = 8
IDX_HEAD_DIM = 64
TOPK_MAX = 256
Q_BLOCK = 128
PAGE_SIZE = 128
MEM_LEN = 256
MEM_N_HEADS = 4
MEM_HEAD_DIM = 256

P_XBC = 0
P_Z = 4096
P_Q = 6144
P_K = 7168
P_V = 7424
P_QI = 7680
P_GSSD = 8192
P_GATT = 9216
P_SMALL = 10240
P_WIDTH = 10368
SM_DT = 0
SM_KI = 32
SM_WI = 96
LANES = 128
SUBLANES = 8

SAMPLE_PAD = 16
SAMPLE_ROWS = 8

VMEM_LIMIT = 56 * 1024 * 1024
NEG_BIG = -1e30
INT_MIN = -2147483648


def _cparams(sem):
    return pltpu.CompilerParams(dimension_semantics=sem, vmem_limit_bytes=VMEM_LIMIT)


def _rms(x, g):
    return x * lax.rsqrt(jnp.mean(x * x, axis=-1, keepdims=True) + EPS) * g


def _sigmoid(x):
    return 1.0 / (1.0 + jnp.exp(-x))


def _silu(x):
    return x * _sigmoid(x)


def _dot(a, b):
    return jnp.dot(a, b, preferred_element_type=F32)


def _dot_nt(a, b):
    return lax.dot_general(a, b, (((1,), (1,)), ((), ())), preferred_element_type=F32)


def _split_bf16(x, parts):
    out = []
    r = x
    for _ in range(parts):
        p = r.astype(BF16)
        out.append(p)
        r = r - p.astype(F32)
    return out


def _ffn_body(x_ref, pg_ref, wg_ref, wu_ref, wd_ref, qg_ref, o_ref, xn_ref, acc_ref):
    j = pl.program_id(1)

    @pl.when(j == 0)
    def _():
        xn_ref[...] = _rms(x_ref[...], pg_ref[...]).astype(BF16)
        acc_ref[...] = jnp.zeros_like(acc_ref)

    xn = xn_ref[...]
    g = _dot(xn, wg_ref[...])
    u = _dot(xn, wu_ref[...])
    a = _silu(g) * u
    acc_ref[...] += _dot(a.astype(BF16), wd_ref[...])

    @pl.when(j == pl.num_programs(1) - 1)
    def _():
        o_ref[...] = x_ref[...] + 0.5 * _rms(acc_ref[...], qg_ref[...])


def _ffn(x, pre_g, wg, wu, wd, post_g, *, tm, th=256):
    m = x.shape[0]
    return pl.pallas_call(
        _ffn_body,
        grid=(m // tm, FFN_HIDDEN // th),
        in_specs=[
            pl.BlockSpec((tm, D_MODEL), lambda i, j: (i, 0)),
            pl.BlockSpec((1, D_MODEL), lambda i, j: (0, 0)),
            pl.BlockSpec((D_MODEL, th), lambda i, j: (0, j)),
            pl.BlockSpec((D_MODEL, th), lambda i, j: (0, j)),
            pl.BlockSpec((th, D_MODEL), lambda i, j: (j, 0)),
            pl.BlockSpec((1, D_MODEL), lambda i, j: (0, 0)),
        ],
        out_specs=pl.BlockSpec((tm, D_MODEL), lambda i, j: (i, 0)),
        out_shape=jax.ShapeDtypeStruct((m, D_MODEL), F32),
        scratch_shapes=[pltpu.VMEM((tm, D_MODEL), BF16), pltpu.VMEM((tm, D_MODEL), F32)],
        compiler_params=_cparams(("parallel", "arbitrary")),
        name="ffn",
    )(x, pre_g, wg, wu, wd, post_g)


def _proj_body(*refs, with_t):
    if with_t:
        x_ref, g_ref, w_ref, wt_ref, o_ref, kt_ref, vt_ref, kit_ref, xn_ref = refs
    else:
        x_ref, g_ref, w_ref, o_ref, xn_ref = refs

    @pl.when(pl.program_id(1) == 0)
    def _():
        xn = _rms(x_ref[...], g_ref[...]).astype(BF16)
        xn_ref[...] = xn
        if with_t:
            t = _dot_nt(wt_ref[...], xn)
            kt_ref[...] = t[0:ATT_KV_W]
            vt_ref[...] = t[ATT_KV_W:2 * ATT_KV_W]
            kit_ref[...] = t[2 * ATT_KV_W:2 * ATT_KV_W + IDX_HEAD_DIM]

    o_ref[...] = _dot(xn_ref[...], w_ref[...])


def _norm_proj(x, g, w, *, tm, tn, wt=None, seq_len=None):
    m, n = x.shape[0], w.shape[1]
    in_specs = [
        pl.BlockSpec((tm, D_MODEL), lambda i, j: (i, 0)),
        pl.BlockSpec((1, D_MODEL), lambda i, j: (0, 0)),
        pl.BlockSpec((D_MODEL, tn), lambda i, j: (0, j)),
    ]
    out_specs = [pl.BlockSpec((tm, tn), lambda i, j: (i, j))]
    out_shape = [jax.ShapeDtypeStruct((m, n), F32)]
    args = [x, g, w]
    if wt is not None:
        per = seq_len // tm
        in_specs.append(pl.BlockSpec(wt.shape, lambda i, j: (0, 0)))
        args.append(wt)
        for rows in (ATT_KV_W, ATT_KV_W, IDX_HEAD_DIM):
            out_specs.append(pl.BlockSpec((None, rows, tm), lambda i, j: (i // per, 0, i % per)))
            out_shape.append(jax.ShapeDtypeStruct((m // seq_len, rows, seq_len), F32))
    res = pl.pallas_call(
        functools.partial(_proj_body, with_t=wt is not None),
        grid=(m // tm, n // tn),
        in_specs=in_specs,
        out_specs=out_specs,
        out_shape=out_shape,
        scratch_shapes=[pltpu.VMEM((tm, D_MODEL), BF16)],
        compiler_params=_cparams(("parallel", "arbitrary")),
        name="norm_proj",
    )(*args)
    return res if wt is not None else res[0]


XP_BASE = 8


def _ssd_body(xbc_ref, z_ref, sm_ref, conv0_ref, ssm0_ref, cw_ref, cb_ref, dtb_ref, alog_ref, dsk_ref,
              ng_ref, tri_ref, exp_ref,
              y_ref, convo_ref, ssmo_ref,
              xp_ref, st_ref, act_ref, dte_ref, acse_ref, *, lb, valid, nc):
    c = pl.program_id(1)
    q = SSD_CHUNK

    @pl.when(c == 0)
    def _():
        if lb < q:
            xp_ref[...] = jnp.zeros_like(xp_ref)
        xp_ref[XP_BASE - 3:XP_BASE, :] = conv0_ref[...]
        for g in range(SSD_N_GROUPS):
            st_ref[g] = ssm0_ref[g * SSD_GROUP_W:(g + 1) * SSD_GROUP_W, :].T

    xp_ref[XP_BASE:XP_BASE + lb, :] = xbc_ref[...]

    cw = 512
    for j in range(SSD_CONV_DIM // cw):
        sl = slice(j * cw, (j + 1) * cw)
        conv = cb_ref[:, sl]
        for t in range(SSD_CONV_W):
            conv = conv + cw_ref[t:t + 1, sl] * xp_ref[XP_BASE - 3 + t:XP_BASE - 3 + t + q, sl]
        act_ref[:, sl] = _silu(conv)

    @pl.when(c == nc - 1)
    def _():
        convo_ref[...] = xp_ref[XP_BASE + valid - 3:XP_BASE + valid, :]

    xp_ref[XP_BASE - 3:XP_BASE, :] = xp_ref[XP_BASE + q - 3:XP_BASE + q, :]

    sm = sm_ref[...]
    if lb < q:
        sm = jnp.concatenate([sm, jnp.zeros((q - lb, LANES), F32)], axis=0)
    row = lax.broadcasted_iota(I32, (q, LANES), 0)
    lane = lax.broadcasted_iota(I32, (q, LANES), 1)
    xdt = sm + dtb_ref[...]
    dt = jnp.maximum(xdt, 0.0) + jnp.log1p(jnp.exp(-jnp.abs(xdt)))
    dt = jnp.where((lane < SSD_N_HEADS) & (row < valid), dt, 0.0)
    a = -jnp.exp(alog_ref[...])
    da = jnp.where(lane < SSD_N_HEADS, dt * a, 0.0)
    tri = tri_ref[...]
    acs = sum(_dot(tri, p) for p in _split_bf16(da, 3))
    acs_t = acs.T
    expand = exp_ref[...]
    dte_ref[...] = sum(_dot(p, expand) for p in _split_bf16(dt, 3))
    acse_ref[...] = sum(_dot(p, expand) for p in _split_bf16(acs, 3))

    tril = row >= lane
    for g in range(SSD_N_GROUPS):
        sl = slice(g * SSD_GROUP_W, (g + 1) * SSD_GROUP_W)
        xs_g = act_ref[:, sl]
        bm_g = act_ref[:, SSD_D_INNER + g * SSD_D_STATE:SSD_D_INNER + (g + 1) * SSD_D_STATE]
        cm_g = act_ref[:, SSD_D_INNER + SSD_N_GROUPS * SSD_D_STATE + g * SSD_D_STATE:
                       SSD_D_INNER + SSD_N_GROUPS * SSD_D_STATE + (g + 1) * SSD_D_STATE]
        acse_g = acse_ref[:, sl]
        alast_g = acse_ref[q - 1:q, sl]
        xdt_g = xs_g * dte_ref[:, sl]
        xw_g = xdt_g * jnp.exp(alast_g - acse_g)
        cmb = cm_g.astype(BF16)
        bmb = bm_g.astype(BF16)
        cb = _dot_nt(cmb, bmb)
        ys = []
        for r in range(SSD_HEADS_PER_GROUP):
            h = g * SSD_HEADS_PER_GROUP + r
            col = jnp.broadcast_to(acs[:, h:h + 1], (q, q))
            rowv = acs_t[h:h + 1, :]
            lmat = jnp.where(tril, jnp.exp(col - rowv), 0.0) * cb
            ys.append(_dot(lmat.astype(BF16),
                           xdt_g[:, r * SSD_HEAD_DIM:(r + 1) * SSD_HEAD_DIM].astype(BF16)))
        y = jnp.concatenate(ys, axis=1)
        st = st_ref[g]
        y = y + _dot(cmb, st.astype(BF16)) * jnp.exp(acse_g)
        st_ref[g] = st * jnp.exp(alast_g) + _dot(bm_g.T.astype(BF16), xw_g.astype(BF16))
        y = y + dsk_ref[:, sl] * xs_g
        yz = y[:lb] * _silu(z_ref[:, sl])
        y_ref[:, sl] = yz * lax.rsqrt(jnp.mean(yz * yz, axis=-1, keepdims=True) + EPS) * ng_ref[:, sl]

    @pl.when(c == nc - 1)
    def _():
        for g in range(SSD_N_GROUPS):
            ssmo_ref[g * SSD_GROUP_W:(g + 1) * SSD_GROUP_W, :] = st_ref[g].T


def _ssd(p3, conv0, ssm0, cw, cb, dtb, alog, dsk, ng, tri, expand, *, lb, valid):
    b, l, _ = p3.shape
    nc = l // lb
    q = SSD_CHUNK
    const = lambda shape: pl.BlockSpec(shape, lambda i, c: (0,) * len(shape))
    return pl.pallas_call(
        functools.partial(_ssd_body, lb=lb, valid=valid, nc=nc),
        grid=(b, nc),
        in_specs=[
            pl.BlockSpec((None, lb, SSD_CONV_DIM), lambda i, c: (i, c, P_XBC // SSD_CONV_DIM)),
            pl.BlockSpec((None, lb, SSD_D_INNER), lambda i, c: (i, c, P_Z // SSD_D_INNER)),
            pl.BlockSpec((None, lb, LANES), lambda i, c: (i, c, P_SMALL // LANES)),
            pl.BlockSpec((None, SSD_CONV_W - 1, SSD_CONV_DIM), lambda i, c: (i, 0, 0)),
            pl.BlockSpec((None, SSD_D_INNER, SSD_D_STATE), lambda i, c: (i, 0, 0)),
            const((SSD_CONV_W, SSD_CONV_DIM)),
            const((1, SSD_CONV_DIM)),
            const((1, LANES)),
            const((1, LANES)),
            const((1, SSD_D_INNER)),
            const((1, SSD_D_INNER)),
            const((q, q)),
            const((LANES, SSD_D_INNER)),
        ],
        out_specs=[
            pl.BlockSpec((None, lb, SSD_D_INNER), lambda i, c: (i, c, 0)),
            pl.BlockSpec((None, SSD_CONV_W - 1, SSD_CONV_DIM), lambda i, c: (i, 0, 0)),
            pl.BlockSpec((None, SSD_D_INNER, SSD_D_STATE), lambda i, c: (i, 0, 0)),
        ],
        out_shape=[
            jax.ShapeDtypeStruct((b, l, SSD_D_INNER), F32),
            jax.ShapeDtypeStruct((b, SSD_CONV_W - 1, SSD_CONV_DIM), F32),
            jax.ShapeDtypeStruct((b, SSD_D_INNER, SSD_D_STATE), F32),
        ],
        scratch_shapes=[
            pltpu.VMEM((XP_BASE + q, SSD_CONV_DIM), F32),
            pltpu.VMEM((SSD_N_GROUPS, SSD_D_STATE, SSD_GROUP_W), F32),
            pltpu.VMEM((q, SSD_CONV_DIM), F32),
            pltpu.VMEM((q, SSD_D_INNER), F32),
            pltpu.VMEM((q, SSD_D_INNER), F32),
        ],
        compiler_params=_cparams(("parallel", "arbitrary")),
        name="ssd",
    )(p3, p3, p3, conv0, ssm0, cw, cb, dtb, alog, dsk, ng, tri, expand)


def _count(pred):
    return jnp.sum(jnp.where(pred, 1.0, 0.0), axis=-1, keepdims=True)


def _topk_mask(score, topk, row_groups=2):
    rows, n = score.shape
    idx_bits = max(1, (n - 1).bit_length())
    score = score + 0.0
    bits = lax.bitcast_convert_type(score, I32)
    key = jnp.where(bits < 0, bits ^ jnp.int32(0x7FFFFFFF), bits)
    kf = float(topk)
    rg = rows // row_groups
    keys = [key[g * rg:(g + 1) * rg] for g in range(row_groups)]

    def vbody(i, ts):
        bit = jnp.int32(1) << (30 - i)
        return tuple(jnp.where(_count(k >= t + bit) >= kf, t + bit, t) for k, t in zip(keys, ts))

    thr0 = tuple(jnp.where(_count(k >= 0) >= kf, jnp.int32(0), jnp.int32(INT_MIN)) for k in keys)
    thr = jnp.concatenate(lax.fori_loop(0, 31, vbody, thr0), axis=0)
    gt = key > thr
    eq = key == thr
    need = kf - _count(gt)
    excess = jnp.max(_count(eq) - need) > 0.0
    idx = lax.broadcasted_iota(I32, score.shape, 1)

    def cut_ties():
        def ibody(i, t):
            cand = t + (jnp.int32(1) << (idx_bits - 1 - i))
            return jnp.where(_count(eq & (idx < cand)) < need, cand, t)
        return lax.fori_loop(0, idx_bits, ibody, jnp.zeros((rows, 1), I32))

    cut = lax.cond(excess, cut_ties, lambda: jnp.full((rows, 1), n, I32))
    return gt | (eq & (idx <= cut))


IDX_DOT_SCALE = IDX_HEAD_DIM ** -0.5
IDX_W_SCALE = IDX_N_HEADS ** -0.5
ATT_SCALE = ATT_HEAD_DIM ** -0.5


CAUSAL_BRANCHES = 4


def _dsa_prompt_block(qb, nk, q_ref, qi_ref, sm_ref, o_ref, kb_ref, vb_ref, kib_ref, topk):
    qi = qi_ref[...]
    wi = sm_ref[:, SM_WI:SM_WI + IDX_N_HEADS] * (IDX_W_SCALE * IDX_DOT_SCALE)
    kib = kib_ref[:, :nk]
    score = jnp.zeros((Q_BLOCK, nk), F32)
    for h in range(IDX_N_HEADS):
        d = _dot(qi[:, h * IDX_HEAD_DIM:(h + 1) * IDX_HEAD_DIM].astype(BF16), kib)
        score = score + wi[:, h:h + 1] * jnp.maximum(d, 0.0)
    qpos = qb * Q_BLOCK + lax.broadcasted_iota(I32, (Q_BLOCK, nk), 0)
    kpos = lax.broadcasted_iota(I32, (Q_BLOCK, nk), 1)
    allowed = kpos <= qpos
    score = jnp.where(allowed, score, -jnp.inf)
    mask = _topk_mask(score, topk) & allowed

    q = (q_ref[...] * ATT_SCALE).astype(BF16)
    outs = []
    for g in range(ATT_N_KV_HEADS):
        kg = kb_ref[g * ATT_HEAD_DIM:(g + 1) * ATT_HEAD_DIM, :nk]
        vg = vb_ref[g * ATT_HEAD_DIM:(g + 1) * ATT_HEAD_DIM, :nk]
        for pair in range(ATT_REP // 2):
            ps, dens = [], []
            for e in range(2):
                hh = g * ATT_REP + pair * 2 + e
                s = _dot(q[:, hh * ATT_HEAD_DIM:(hh + 1) * ATT_HEAD_DIM], kg)
                s = jnp.where(mask, s, -jnp.inf)
                p = jnp.exp(s - jnp.max(s, axis=-1, keepdims=True))
                dens.append(jnp.sum(p, axis=-1, keepdims=True))
                ps.append(p.astype(BF16))
            o = _dot_nt(vg, jnp.concatenate(ps, axis=0)).T
            outs.append(o[0:Q_BLOCK] / dens[0])
            outs.append(o[Q_BLOCK:2 * Q_BLOCK] / dens[1])
    o_ref[...] = jnp.concatenate(outs, axis=1)


def _dsa_prompt_body(q_ref, qi_ref, sm_ref, kt_ref, vt_ref, kit_ref, o_ref, kb_ref, vb_ref, kib_ref,
                     *, topk, key_lens, blocks_per_len):
    qb = pl.program_id(1)

    @pl.when(qb == 0)
    def _():
        kb_ref[...] = kt_ref[...].astype(BF16)
        vb_ref[...] = vt_ref[...].astype(BF16)
        kib_ref[...] = kit_ref[...].astype(BF16)

    for i, nk in enumerate(key_lens):
        @pl.when(qb // blocks_per_len == i)
        def _(nk=nk):
            _dsa_prompt_block(qb, nk, q_ref, qi_ref, sm_ref, o_ref, kb_ref, vb_ref, kib_ref, topk)


def _dsa_prompt(p3, kt, vt, kit):
    b, l, _ = p3.shape
    topk = min(TOPK_MAX, l // 4)
    aw = ATT_N_HEADS * ATT_HEAD_DIM
    qiw = IDX_N_HEADS * IDX_HEAD_DIM
    nblk = l // Q_BLOCK
    branches = math.gcd(nblk, CAUSAL_BRANCHES)
    blocks_per_len = nblk // branches
    key_lens = tuple((i + 1) * blocks_per_len * Q_BLOCK for i in range(branches))
    return pl.pallas_call(
        functools.partial(_dsa_prompt_body, topk=topk, key_lens=key_lens,
                          blocks_per_len=blocks_per_len),
        grid=(b, nblk),
        in_specs=[
            pl.BlockSpec((None, Q_BLOCK, aw), lambda i, j: (i, j, P_Q // aw)),
            pl.BlockSpec((None, Q_BLOCK, qiw), lambda i, j: (i, j, P_QI // qiw)),
            pl.BlockSpec((None, Q_BLOCK, LANES), lambda i, j: (i, j, P_SMALL // LANES)),
            pl.BlockSpec((None, ATT_KV_W, l), lambda i, j: (i, 0, 0)),
            pl.BlockSpec((None, ATT_KV_W, l), lambda i, j: (i, 0, 0)),
            pl.BlockSpec((None, IDX_HEAD_DIM, l), lambda i, j: (i, 0, 0)),
        ],
        out_specs=pl.BlockSpec((None, Q_BLOCK, aw), lambda i, j: (i, j, 0)),
        out_shape=jax.ShapeDtypeStruct((b, l, aw), F32),
        scratch_shapes=[
            pltpu.VMEM((ATT_KV_W, l), BF16),
            pltpu.VMEM((ATT_KV_W, l), BF16),
            pltpu.VMEM((IDX_HEAD_DIM, l), BF16),
        ],
        compiler_params=_cparams(("parallel", "arbitrary")),
        name="dsa_prompt",
    )(p3, p3, p3, kt, vt, kit)


SCORE_KEYS = 2048


def _dsa_sample_score_body(pt_ref, qi_ref, sm_ref, kidx_hbm, o_ref, kbuf, sem, *, n_pages, valid):
    b = pl.program_id(0)
    nb = pl.num_programs(0)
    past = n_pages * PAGE_SIZE

    def page_copy(bb, pg, slot):
        return pltpu.make_async_copy(kidx_hbm.at[pt_ref[bb, pg]],
                                     kbuf.at[slot, :, pl.ds(pg * PAGE_SIZE, PAGE_SIZE)],
                                     sem.at[slot])

    def fetch(bb, slot):
        def body(pg, carry):
            page_copy(bb, pg, slot).start()
            return carry
        lax.fori_loop(0, n_pages, body, 0)

    @pl.when(b == 0)
    def _():
        fetch(0, 0)

    @pl.when(b + 1 < nb)
    def _():
        fetch(b + 1, (b + 1) % 2)

    slot = b % 2

    def wbody(pg, carry):
        page_copy(b, pg, slot).wait()
        return carry
    lax.fori_loop(0, n_pages, wbody, 0)

    qi = qi_ref[...]
    qh = jnp.concatenate([qi[:, h * IDX_HEAD_DIM:(h + 1) * IDX_HEAD_DIM] for h in range(IDX_N_HEADS)],
                         axis=0).astype(BF16)
    wi = sm_ref[:, SM_WI:SM_WI + IDX_N_HEADS]
    wcol = (jnp.concatenate([wi[:, h:h + 1] for h in range(IDX_N_HEADS)], axis=0)
            * (IDX_W_SCALE * IDX_DOT_SCALE))

    def head_sum(d):
        w = wcol * jnp.maximum(d, 0.0)
        sc = w[0:SAMPLE_PAD]
        for h in range(1, IDX_N_HEADS):
            sc = sc + w[h * SAMPLE_PAD:(h + 1) * SAMPLE_PAD]
        return sc

    for kc in range(past // SCORE_KEYS):
        kk = kbuf[slot, :, kc * SCORE_KEYS:(kc + 1) * SCORE_KEYS].astype(BF16)
        sc = head_sum(_dot(qh, kk))
        o_ref[:, kc * SCORE_KEYS:(kc + 1) * SCORE_KEYS] = sc[0:SAMPLE_ROWS]

    kin = sm_ref[:, SM_KI:SM_KI + IDX_HEAD_DIM]
    kin = jnp.concatenate([kin, jnp.zeros((LANES - SAMPLE_PAD, IDX_HEAD_DIM), F32)], axis=0).astype(BF16)
    sc = head_sum(_dot_nt(qh, kin))
    trow = lax.broadcasted_iota(I32, (SAMPLE_PAD, LANES), 0)
    tcol = lax.broadcasted_iota(I32, (SAMPLE_PAD, LANES), 1)
    sc = jnp.where((tcol <= trow) & (tcol < valid), sc, -jnp.inf)
    o_ref[:, past:past + LANES] = sc[0:SAMPLE_ROWS]


def _dsa_sample_score(p3, kidx_pool, page_table, *, valid):
    b = p3.shape[0]
    n_pages = page_table.shape[1]
    past = n_pages * PAGE_SIZE
    qiw = IDX_N_HEADS * IDX_HEAD_DIM
    grid_spec = pltpu.PrefetchScalarGridSpec(
        num_scalar_prefetch=1,
        grid=(b,),
        in_specs=[
            pl.BlockSpec((None, SAMPLE_PAD, qiw), lambda i, pt: (i, 0, P_QI // qiw)),
            pl.BlockSpec((None, SAMPLE_PAD, LANES), lambda i, pt: (i, 0, P_SMALL // LANES)),
            pl.BlockSpec(memory_space=pl.ANY),
        ],
        out_specs=pl.BlockSpec((None, SAMPLE_ROWS, past + LANES), lambda i, pt: (i, 0, 0)),
        scratch_shapes=[
            pltpu.VMEM((2, IDX_HEAD_DIM, past), F32),
            pltpu.SemaphoreType.DMA((2,)),
        ],
    )
    return pl.pallas_call(
        functools.partial(_dsa_sample_score_body, n_pages=n_pages, valid=valid),
        grid_spec=grid_spec,
        out_shape=jax.ShapeDtypeStruct((b, SAMPLE_ROWS, past + LANES), F32),
        compiler_params=_cparams(("arbitrary",)),
        name="dsa_sample_score",
    )(page_table, p3, p3, kidx_pool)


def _select_body(s_ref, o_ref, *, topk):
    score = s_ref[...]
    mask = _topk_mask(score, topk) & (score > -jnp.inf)
    o_ref[...] = jnp.where(mask, 1.0, 0.0)


def _select(score2d, *, topk, tr):
    r, n = score2d.shape
    return pl.pallas_call(
        functools.partial(_select_body, topk=topk),
        grid=(r // tr,),
        in_specs=[pl.BlockSpec((tr, n), lambda i: (i, 0))],
        out_specs=pl.BlockSpec((tr, n), lambda i: (i, 0)),
        out_shape=jax.ShapeDtypeStruct((r, n), F32),
        compiler_params=_cparams(("parallel",)),
        name="dsa_sample_select",
    )(score2d)


ATT_ROWS = ATT_N_HEADS * SAMPLE_ROWS


def _dsa_sample_attn_body(pt_ref, q_ref, kn_ref, vn_ref, mask_ref, maskn_ref, kpool, vpool, o_ref,
                          kbuf, vbuf, sem, qbd_ref, m_ref, l_ref, acc_ref, *, n_chunks, ppc):
    b = pl.program_id(0)
    c = pl.program_id(1)
    nb = pl.num_programs(0)
    step = b * n_chunks + c

    def page_copies(bb, cc, j, slot):
        pg = pt_ref[bb, cc * ppc + j]
        dst = pl.ds(j * PAGE_SIZE, PAGE_SIZE)
        return (pltpu.make_async_copy(kpool.at[pg], kbuf.at[slot, :, dst], sem.at[0, slot]),
                pltpu.make_async_copy(vpool.at[pg], vbuf.at[slot, :, dst], sem.at[1, slot]))

    def fetch(st, slot):
        bb = st // n_chunks
        cc = st % n_chunks
        for j in range(ppc):
            for cp in page_copies(bb, cc, j, slot):
                cp.start()

    @pl.when(step == 0)
    def _():
        fetch(0, 0)

    @pl.when(step + 1 < nb * n_chunks)
    def _():
        fetch(step + 1, (step + 1) % 2)

    slot = step % 2
    for j in range(ppc):
        for cp in page_copies(b, c, j, slot):
            cp.wait()

    @pl.when(c == 0)
    def _():
        q = q_ref[...]
        qbd_ref[...] = jnp.zeros_like(qbd_ref)
        for g in range(ATT_N_KV_HEADS):
            for r in range(ATT_REP):
                hh = g * ATT_REP + r
                r0 = (g * ATT_REP + r) * SAMPLE_ROWS
                qbd_ref[r0:r0 + SAMPLE_ROWS, g * ATT_HEAD_DIM:(g + 1) * ATT_HEAD_DIM] = (
                    q[0:SAMPLE_ROWS, hh * ATT_HEAD_DIM:(hh + 1) * ATT_HEAD_DIM])
        m_ref[...] = jnp.full_like(m_ref, NEG_BIG)
        l_ref[...] = jnp.zeros_like(l_ref)
        acc_ref[...] = jnp.zeros_like(acc_ref)

    qbd = (qbd_ref[...] * ATT_SCALE).astype(BF16)

    def update(kt, vt, msk8):
        s = _dot(qbd, kt)
        mk = jnp.tile(msk8, (ATT_N_HEADS, 1)) > 0.0
        m_old = m_ref[...]
        m_new = jnp.maximum(m_old, jnp.max(jnp.where(mk, s, NEG_BIG), axis=-1, keepdims=True))
        p = jnp.where(mk, jnp.exp(s - m_new), 0.0)
        alpha = jnp.exp(m_old - m_new)
        l_ref[...] = alpha * l_ref[...] + jnp.sum(p, axis=-1, keepdims=True)
        acc_ref[...] = alpha * acc_ref[...] + _dot_nt(p.astype(BF16), vt)
        m_ref[...] = m_new

    update(kbuf[slot].astype(BF16), vbuf[slot].astype(BF16), mask_ref[...])

    @pl.when(c == n_chunks - 1)
    def _():
        pad = jnp.zeros((LANES - SAMPLE_PAD, ATT_KV_W), F32)
        kn = jnp.concatenate([kn_ref[...], pad], axis=0).T.astype(BF16)
        vn = jnp.concatenate([vn_ref[...], pad], axis=0).T.astype(BF16)
        update(kn, vn, maskn_ref[...])
        o = acc_ref[...] / l_ref[...]
        o_ref[...] = jnp.zeros_like(o_ref)
        for g in range(ATT_N_KV_HEADS):
            for r in range(ATT_REP):
                hh = g * ATT_REP + r
                r0 = (g * ATT_REP + r) * SAMPLE_ROWS
                o_ref[0:SAMPLE_ROWS, hh * ATT_HEAD_DIM:(hh + 1) * ATT_HEAD_DIM] = (
                    o[r0:r0 + SAMPLE_ROWS, g * ATT_HEAD_DIM:(g + 1) * ATT_HEAD_DIM])


def _dsa_sample_attn(p3, mask3, kpool, vpool, page_table, *, ppc=16):
    b = p3.shape[0]
    n_pages = page_table.shape[1]
    n_chunks = n_pages // ppc
    ck = ppc * PAGE_SIZE
    past = n_pages * PAGE_SIZE
    aw = ATT_N_HEADS * ATT_HEAD_DIM
    grid_spec = pltpu.PrefetchScalarGridSpec(
        num_scalar_prefetch=1,
        grid=(b, n_chunks),
        in_specs=[
            pl.BlockSpec((None, SAMPLE_PAD, aw), lambda i, c, pt: (i, 0, P_Q // aw)),
            pl.BlockSpec((None, SAMPLE_PAD, ATT_KV_W), lambda i, c, pt: (i, 0, P_K // ATT_KV_W)),
            pl.BlockSpec((None, SAMPLE_PAD, ATT_KV_W), lambda i, c, pt: (i, 0, P_V // ATT_KV_W)),
            pl.BlockSpec((None, SAMPLE_ROWS, ck), lambda i, c, pt: (i, 0, c)),
            pl.BlockSpec((None, SAMPLE_ROWS, LANES), lambda i, c, pt: (i, 0, past // LANES)),
            pl.BlockSpec(memory_space=pl.ANY),
            pl.BlockSpec(memory_space=pl.ANY),
        ],
        out_specs=pl.BlockSpec((None, SAMPLE_PAD, aw), lambda i, c, pt: (i, 0, 0)),
        scratch_shapes=[
            pltpu.VMEM((2, ATT_KV_W, ck), F32),
            pltpu.VMEM((2, ATT_KV_W, ck), F32),
            pltpu.SemaphoreType.DMA((2, 2)),
            pltpu.VMEM((ATT_ROWS, ATT_KV_W), F32),
            pltpu.VMEM((ATT_ROWS, 1), F32),
            pltpu.VMEM((ATT_ROWS, 1), F32),
            pltpu.VMEM((ATT_ROWS, ATT_KV_W), F32),
        ],
    )
    return pl.pallas_call(
        functools.partial(_dsa_sample_attn_body, n_chunks=n_chunks, ppc=ppc),
        grid_spec=grid_spec,
        out_shape=jax.ShapeDtypeStruct((b, SAMPLE_PAD, aw), F32),
        compiler_params=_cparams(("arbitrary", "arbitrary")),
        name="dsa_sample_attn",
    )(page_table, p3, p3, p3, mask3, mask3, kpool, vpool)


def _merge_body(h_ref, ys_ref, ya_ref, gs_ref, ga_ref, wbs_ref, wba_ref, wo_ref, pg_ref, o_ref):
    a = _dot(ys_ref[...].astype(BF16), wbs_ref[...])
    bb = _dot(ya_ref[...].astype(BF16), wba_ref[...])
    merged = _sigmoid(gs_ref[...]) * a + _sigmoid(ga_ref[...]) * bb
    o_ref[...] = h_ref[...] + _rms(_dot(merged.astype(BF16), wo_ref[...]), pg_ref[...])


def _merge(h, y_ssd, y_att, p2, wbs, wba, wo, post_g, *, tm):
    m = h.shape[0]
    const = lambda shape: pl.BlockSpec(shape, lambda i: (0,) * len(shape))
    return pl.pallas_call(
        _merge_body,
        grid=(m // tm,),
        in_specs=[
            pl.BlockSpec((tm, D_MODEL), lambda i: (i, 0)),
            pl.BlockSpec((tm, SSD_D_INNER), lambda i: (i, 0)),
            pl.BlockSpec((tm, D_MODEL), lambda i: (i, 0)),
            pl.BlockSpec((tm, D_MODEL), lambda i: (i, P_GSSD // D_MODEL)),
            pl.BlockSpec((tm, D_MODEL), lambda i: (i, P_GATT // D_MODEL)),
            const((SSD_D_INNER, D_MODEL)),
            const((D_MODEL, D_MODEL)),
            const((D_MODEL, D_MODEL)),
            const((1, D_MODEL)),
        ],
        out_specs=pl.BlockSpec((tm, D_MODEL), lambda i: (i, 0)),
        out_shape=jax.ShapeDtypeStruct((m, D_MODEL), F32),
        compiler_params=_cparams(("parallel",)),
        name="merge",
    )(h, y_ssd, y_att, p2, p2, wbs, wba, wo, post_g)


MEM_SCALE = MEM_HEAD_DIM ** -0.5


def _mem_body(h_ref, mk_ref, mv_ref, pg_ref, wq_ref, wo_ref, qg_ref, o_ref):
    h = h_ref[...]
    q = _dot(_rms(h, pg_ref[...]).astype(BF16), wq_ref[...])
    outs = []
    for hd in range(MEM_N_HEADS):
        sl = slice(hd * MEM_HEAD_DIM, (hd + 1) * MEM_HEAD_DIM)
        s = _dot_nt(q[:, sl].astype(BF16), mk_ref[:, sl].astype(BF16)) * MEM_SCALE
        p = jnp.exp(s - jnp.max(s, axis=-1, keepdims=True))
        den = jnp.sum(p, axis=-1, keepdims=True)
        outs.append(_dot(p.astype(BF16), mv_ref[:, sl].astype(BF16)) / den)
    att = jnp.concatenate(outs, axis=1)
    o_ref[...] = h + _rms(_dot(att.astype(BF16), wo_ref[...]), qg_ref[...])


def _mem_attn(h3, mk, mv, pre_g, wq, wo, post_g, *, tl):
    b, l, _ = h3.shape
    const = lambda shape: pl.BlockSpec(shape, lambda i, j: (0,) * len(shape))
    return pl.pallas_call(
        _mem_body,
        grid=(b, l // tl),
        in_specs=[
            pl.BlockSpec((None, tl, D_MODEL), lambda i, j: (i, j, 0)),
            pl.BlockSpec((None, MEM_LEN, D_MODEL), lambda i, j: (i, 0, 0)),
            pl.BlockSpec((None, MEM_LEN, D_MODEL), lambda i, j: (i, 0, 0)),
            const((1, D_MODEL)),
            const((D_MODEL, D_MODEL)),
            const((D_MODEL, D_MODEL)),
            const((1, D_MODEL)),
        ],
        out_specs=pl.BlockSpec((None, tl, D_MODEL), lambda i, j: (i, j, 0)),
        out_shape=jax.ShapeDtypeStruct((b, l, D_MODEL), F32),
        compiler_params=_cparams(("parallel", "parallel")),
        name="mem_attn",
    )(h3, mk, mv, pre_g, wq, wo, post_g)


def _prep_weights(w):
    wi = w["w_in"]
    offs = {}
    o = 0
    for name, width in (("z", SSD_D_INNER), ("xbc", SSD_CONV_DIM), ("dt", SSD_N_HEADS),
                        ("q", ATT_N_HEADS * ATT_HEAD_DIM), ("k", ATT_KV_W), ("v", ATT_KV_W),
                        ("qi", IDX_N_HEADS * IDX_HEAD_DIM), ("ki", IDX_HEAD_DIM), ("wi", IDX_N_HEADS),
                        ("gs", D_MODEL), ("ga", D_MODEL)):
        offs[name] = (o, o + width)
        o += width
    seg = lambda n: wi[:, offs[n][0]:offs[n][1]]
    pad = jnp.zeros((D_MODEL, P_WIDTH - P_SMALL - SSD_N_HEADS - IDX_HEAD_DIM - IDX_N_HEADS), wi.dtype)
    w_in = jnp.concatenate([seg("xbc"), seg("z"), seg("q"), seg("k"), seg("v"), seg("qi"), seg("gs"),
                            seg("ga"), seg("dt"), seg("ki"), seg("wi"), pad], axis=1).astype(BF16)
    w_kv_t = jnp.concatenate([seg("k"), seg("v"), seg("ki")], axis=1).T.astype(BF16)
    lane_pad = lambda v: jnp.pad(v, (0, LANES - v.shape[0]))[None, :]
    head = jnp.arange(SSD_D_INNER) // SSD_HEAD_DIM
    out = dict(w)
    out.update(
        w_in=w_in,
        w_kv_t=w_kv_t,
        dt_bias=lane_pad(w["dt_bias"]),
        a_log=lane_pad(w["a_log"]),
        d_skip=jnp.broadcast_to(w["d_skip"][:, None], (SSD_N_HEADS, SSD_HEAD_DIM)).reshape(1, SSD_D_INNER),
        tri=(jnp.arange(SSD_CHUNK)[:, None] >= jnp.arange(SSD_CHUNK)[None, :]).astype(BF16),
        expand=(jnp.arange(LANES)[:, None] == head[None, :]).astype(BF16),
    )
    for name in ("ffn1_wg", "ffn1_wu", "ffn1_wd", "ffn2_wg", "ffn2_wu", "ffn2_wd", "w_br_ssd", "w_br_att",
                 "w_out", "w_mq", "w_mk", "w_mv", "w_mo"):
        out[name] = w[name].astype(BF16)
    for name in ("ffn1_pre_g", "ffn1_post_g", "mix_pre_g", "mix_post_g", "mem_pre_g", "mem_kv_g",
                 "mem_post_g", "ffn2_pre_g", "ffn2_post_g", "conv_b", "ssd_norm_g"):
        out[name] = w[name][None, :]
    return out


def _layer(x3, w, conv0, ssm0, mem_k, mem_v, attn_fn, *, tm, tn, lb, valid, tl, kv_transposed):
    b, l, _ = x3.shape
    x = x3.reshape(b * l, D_MODEL)
    h = _ffn(x, w["ffn1_pre_g"], w["ffn1_wg"], w["ffn1_wu"], w["ffn1_wd"], w["ffn1_post_g"], tm=tm)
    if kv_transposed:
        p2, *kv_t = _norm_proj(h, w["mix_pre_g"], w["w_in"], tm=tm, tn=tn, wt=w["w_kv_t"], seq_len=l)
    else:
        p2, kv_t = _norm_proj(h, w["mix_pre_g"], w["w_in"], tm=tm, tn=tn), None
    p3 = p2.reshape(b, l, P_WIDTH)
    y_ssd, conv_new, ssm_new = _ssd(p3, conv0, ssm0, w["conv_w"], w["conv_b"], w["dt_bias"], w["a_log"],
                                    w["d_skip"], w["ssd_norm_g"], w["tri"], w["expand"], lb=lb, valid=valid)
    y_att = attn_fn(p3, *kv_t) if kv_transposed else attn_fn(p3)
    h = _merge(h, y_ssd.reshape(b * l, SSD_D_INNER), y_att.reshape(b * l, D_MODEL), p2,
               w["w_br_ssd"], w["w_br_att"], w["w_out"], w["mix_post_g"], tm=min(tm, 512))
    h = _mem_attn(h.reshape(b, l, D_MODEL), mem_k, mem_v, w["mem_pre_g"], w["w_mq"], w["w_mo"],
                  w["mem_post_g"], tl=tl).reshape(b * l, D_MODEL)
    h = _ffn(h, w["ffn2_pre_g"], w["ffn2_wg"], w["ffn2_wu"], w["ffn2_wd"], w["ffn2_post_g"], tm=tm)
    return h.reshape(b, l, D_MODEL), p3, kv_t, conv_new, ssm_new


def _kv_rows(p3, n):
    b = p3.shape[0]
    k = p3[:, :n, P_K:P_K + ATT_KV_W].reshape(1, b, n, ATT_N_KV_HEADS, ATT_HEAD_DIM)
    v = p3[:, :n, P_V:P_V + ATT_KV_W].reshape(1, b, n, ATT_N_KV_HEADS, ATT_HEAD_DIM)
    ki = p3[:, :n, P_SMALL + SM_KI:P_SMALL + SM_KI + IDX_HEAD_DIM][None]
    return k, v, ki


def _kv_rows_from_transposed(kt, vt, kit):
    b, _, l = kt.shape
    heads = lambda t: t.reshape(b, ATT_N_KV_HEADS, ATT_HEAD_DIM, l).transpose(0, 3, 1, 2)[None]
    return heads(kt), heads(vt), kit.transpose(0, 2, 1)[None]


def kernel(x_prompt, x_sample, mem_prompt, cache_k, cache_v, cache_kidx, state_ssm, state_conv, cache_mem_k, cache_mem_v, page_table, ffn1_pre_g, ffn1_wg, ffn1_wu, ffn1_wd, ffn1_post_g, mix_pre_g, w_in, conv_w, conv_b, dt_bias, a_log, d_skip, ssd_norm_g, w_br_ssd, w_br_att, w_out, mix_post_g, mem_pre_g, mem_kv_g, w_mq, w_mk, w_mv, w_mo, mem_post_g, ffn2_pre_g, ffn2_wg, ffn2_wu, ffn2_wd, ffn2_post_g):
    assert ffn1_wg.shape[0] == 1, "one trunk layer"
    w = _prep_weights(dict(
        ffn1_pre_g=ffn1_pre_g[0], ffn1_wg=ffn1_wg[0], ffn1_wu=ffn1_wu[0], ffn1_wd=ffn1_wd[0],
        ffn1_post_g=ffn1_post_g[0], mix_pre_g=mix_pre_g[0], w_in=w_in[0], conv_w=conv_w[0], conv_b=conv_b[0],
        dt_bias=dt_bias[0], a_log=a_log[0], d_skip=d_skip[0], ssd_norm_g=ssd_norm_g[0], w_br_ssd=w_br_ssd[0],
        w_br_att=w_br_att[0], w_out=w_out[0], mix_post_g=mix_post_g[0], mem_pre_g=mem_pre_g[0],
        mem_kv_g=mem_kv_g[0], w_mq=w_mq[0], w_mk=w_mk[0], w_mv=w_mv[0], w_mo=w_mo[0],
        mem_post_g=mem_post_g[0], ffn2_pre_g=ffn2_pre_g[0], ffn2_wg=ffn2_wg[0], ffn2_wu=ffn2_wu[0],
        ffn2_wd=ffn2_wd[0], ffn2_post_g=ffn2_post_g[0]))

    bp, lp, _ = x_prompt.shape
    mem2 = mem_prompt.reshape(bp * MEM_LEN, D_MODEL)
    mk_p = _norm_proj(mem2, w["mem_kv_g"], w["w_mk"], tm=512, tn=D_MODEL).reshape(bp, MEM_LEN, D_MODEL)
    mv_p = _norm_proj(mem2, w["mem_kv_g"], w["w_mv"], tm=512, tn=D_MODEL).reshape(bp, MEM_LEN, D_MODEL)
    conv0 = jnp.zeros((bp, SSD_CONV_W - 1, SSD_CONV_DIM), F32)
    ssm0 = jnp.zeros((bp, SSD_D_INNER, SSD_D_STATE), F32)
    y_p, _, kv_t, conv_p, ssm_p = _layer(x_prompt, w, conv0, ssm0, mk_p, mv_p, _dsa_prompt, tm=1024, tn=1152,
                                         lb=SSD_CHUNK, valid=SSD_CHUNK, tl=512, kv_transposed=True)
    k_p, v_p, ki_p = _kv_rows_from_transposed(*kv_t)

    bs, ls, _ = x_sample.shape
    assert ls <= SAMPLE_ROWS
    n_pages = page_table.shape[1]
    past = n_pages * PAGE_SIZE
    n_pool = cache_k.shape[1]
    kpool = cache_k[0].transpose(0, 2, 3, 1).reshape(n_pool, ATT_KV_W, PAGE_SIZE)
    vpool = cache_v[0].transpose(0, 2, 3, 1).reshape(n_pool, ATT_KV_W, PAGE_SIZE)
    kidx_pool = cache_kidx[0].transpose(0, 2, 1)
    topk_s = min(TOPK_MAX, (past + ls) // 4)

    def sample_attn(p3):
        score = _dsa_sample_score(p3, kidx_pool, page_table, valid=ls)
        mask = _select(score.reshape(bs * SAMPLE_ROWS, past + LANES), topk=topk_s, tr=32)
        return _dsa_sample_attn(p3, mask.reshape(bs, SAMPLE_ROWS, past + LANES), kpool, vpool, page_table)

    xs = jnp.pad(x_sample, ((0, 0), (0, SAMPLE_PAD - ls), (0, 0)))
    y_s, p3_s, _, conv_s, ssm_s = _layer(
        xs, w, state_conv[0], state_ssm[0].reshape(bs, SSD_D_INNER, SSD_D_STATE),
        cache_mem_k[0].reshape(bs, MEM_LEN, D_MODEL), cache_mem_v[0].reshape(bs, MEM_LEN, D_MODEL),
        sample_attn, tm=bs * SAMPLE_PAD, tn=1152, lb=SAMPLE_PAD, valid=ls, tl=SAMPLE_PAD,
        kv_transposed=False)
    k_s, v_s, ki_s = _kv_rows(p3_s, ls)

    ssm_shape = (1, -1, SSD_N_HEADS, SSD_HEAD_DIM, SSD_D_STATE)
    mem_shape = (1, bp, MEM_LEN, MEM_N_HEADS, MEM_HEAD_DIM)
    return (y_p, y_s[:, :ls], k_p, v_p, ki_p, ssm_p.reshape(ssm_shape), conv_p[None],
            mk_p.reshape(mem_shape), mv_p.reshape(mem_shape),
            k_s, v_s, ki_s, ssm_s.reshape(ssm_shape), conv_s[None])
```

```python
import functools
import math

import jax
import jax.numpy as jnp
from jax import lax
from jax.experimental import pallas as pl
from jax.experimental.pallas import tpu as pltpu

F32 = jnp.float32
BF16 = jnp.bfloat16
I32 = jnp.int32

D_MODEL = 1024
EPS = 1e-6
FFN_HIDDEN = 2816
SSD_D_INNER = 2048
SSD_HEAD_DIM = 64
SSD_N_HEADS = 32
SSD_N_GROUPS = 8
SSD_D_STATE = 128
SSD_CONV_W = 4
SSD_CONV_DIM = 4096
SSD_CHUNK = 128
SSD_GROUP_W = SSD_D_INNER // SSD_N_GROUPS
SSD_HEADS_PER_GROUP = SSD_N_HEADS // SSD_N_GROUPS
ATT_N_HEADS = 16
ATT_N_KV_HEADS = 4
ATT_HEAD_DIM = 64
ATT_REP = ATT_N_HEADS // ATT_N_KV_HEADS
ATT_KV_W = ATT_N_KV_HEADS * ATT_HEAD_DIM
IDX_N_HEADS = 8
IDX_HEAD_DIM = 64
TOPK_MAX = 256
Q_BLOCK = 128
PAGE_SIZE = 128
MEM_LEN = 256
MEM_N_HEADS = 4
MEM_HEAD_DIM = 256

P_XBC = 0
P_Z = 4096
P_Q = 6144
P_K = 7168
P_V = 7424
P_QI = 7680
P_GSSD = 8192
P_GATT = 9216
P_SMALL = 10240
P_WIDTH = 10368
SM_DT = 0
SM_KI = 32
SM_WI = 96
LANES = 128
SUBLANES = 8

SAMPLE_PAD = 16
SAMPLE_ROWS = 8

VMEM_LIMIT = 56 * 1024 * 1024
NEG_BIG = -1e30


def _cparams(sem):
    return pltpu.CompilerParams(dimension_semantics=sem, vmem_limit_bytes=VMEM_LIMIT)


def _rms(x, g):
    return x * lax.rsqrt(jnp.mean(x * x, axis=-1, keepdims=True) + EPS) * g


def _sigmoid(x):
    return 1.0 / (1.0 + jnp.exp(-x))


def _silu(x):
    return x * _sigmoid(x)


def _dot(a, b):
    return jnp.dot(a, b, preferred_element_type=F32)


def _dot_nt(a, b):
    return lax.dot_general(a, b, (((1,), (1,)), ((), ())), preferred_element_type=F32)


def _split_bf16(x, parts):
    out = []
    r = x
    for _ in range(parts):
        p = r.astype(BF16)
        out.append(p)
        r = r - p.astype(F32)
    return out


FFN_TH = 256


def _ffn_body(x_ref, pg_ref, wgu_ref, wd_ref, qg_ref, o_ref, xn_ref, acc_ref):
    j = pl.program_id(1)

    @pl.when(j == 0)
    def _():
        xn_ref[...] = _rms(x_ref[...], pg_ref[...]).astype(BF16)
        acc_ref[...] = jnp.zeros_like(acc_ref)

    gu = _dot(xn_ref[...], wgu_ref[...])
    a = _silu(gu[:, :FFN_TH]) * gu[:, FFN_TH:]
    acc_ref[...] += _dot(a.astype(BF16), wd_ref[...])

    @pl.when(j == pl.num_programs(1) - 1)
    def _():
        o_ref[...] = x_ref[...] + 0.5 * _rms(acc_ref[...], qg_ref[...])


def _ffn(x, pre_g, wgu, wd, post_g, *, tm):
    m = x.shape[0]
    return pl.pallas_call(
        _ffn_body,
        grid=(m // tm, FFN_HIDDEN // FFN_TH),
        in_specs=[
            pl.BlockSpec((tm, D_MODEL), lambda i, j: (i, 0)),
            pl.BlockSpec((1, D_MODEL), lambda i, j: (0, 0)),
            pl.BlockSpec((None, D_MODEL, 2 * FFN_TH), lambda i, j: (j, 0, 0)),
            pl.BlockSpec((FFN_TH, D_MODEL), lambda i, j: (j, 0)),
            pl.BlockSpec((1, D_MODEL), lambda i, j: (0, 0)),
        ],
        out_specs=pl.BlockSpec((tm, D_MODEL), lambda i, j: (i, 0)),
        out_shape=jax.ShapeDtypeStruct((m, D_MODEL), F32),
        scratch_shapes=[pltpu.VMEM((tm, D_MODEL), BF16), pltpu.VMEM((tm, D_MODEL), F32)],
        compiler_params=_cparams(("parallel", "arbitrary")),
        name="ffn",
    )(x, pre_g, wgu, wd, post_g)


def _proj_body(*refs, with_t):
    if with_t:
        x_ref, g_ref, w_ref, wt_ref, o_ref, kt_ref, vt_ref, kit_ref, xn_ref = refs
    else:
        x_ref, g_ref, w_ref, o_ref, xn_ref = refs

    @pl.when(pl.program_id(1) == 0)
    def _():
        xn = _rms(x_ref[...], g_ref[...]).astype(BF16)
        xn_ref[...] = xn
        if with_t:
            t = _dot_nt(wt_ref[...], xn)
            kt_ref[...] = t[0:ATT_KV_W]
            vt_ref[...] = t[ATT_KV_W:2 * ATT_KV_W]
            kit_ref[...] = t[2 * ATT_KV_W:2 * ATT_KV_W + IDX_HEAD_DIM]

    o_ref[...] = _dot(xn_ref[...], w_ref[...])


def _norm_proj(x, g, w, *, tm, tn, wt=None, seq_len=None):
    m, n = x.shape[0], w.shape[1]
    in_specs = [
        pl.BlockSpec((tm, D_MODEL), lambda i, j: (i, 0)),
        pl.BlockSpec((1, D_MODEL), lambda i, j: (0, 0)),
        pl.BlockSpec((D_MODEL, tn), lambda i, j: (0, j)),
    ]
    out_specs = [pl.BlockSpec((tm, tn), lambda i, j: (i, j))]
    out_shape = [jax.ShapeDtypeStruct((m, n), F32)]
    args = [x, g, w]
    if wt is not None:
        per = seq_len // tm
        in_specs.append(pl.BlockSpec(wt.shape, lambda i, j: (0, 0)))
        args.append(wt)
        for rows in (ATT_KV_W, ATT_KV_W, IDX_HEAD_DIM):
            out_specs.append(pl.BlockSpec((None, rows, tm), lambda i, j: (i // per, 0, i % per)))
            out_shape.append(jax.ShapeDtypeStruct((m // seq_len, rows, seq_len), F32))
    res = pl.pallas_call(
        functools.partial(_proj_body, with_t=wt is not None),
        grid=(m // tm, n // tn),
        in_specs=in_specs,
        out_specs=out_specs,
        out_shape=out_shape,
        scratch_shapes=[pltpu.VMEM((tm, D_MODEL), BF16)],
        compiler_params=_cparams(("parallel", "arbitrary")),
        name="norm_proj",
    )(*args)
    return res if wt is not None else res[0]


XP_BASE = 8


def _ssd_body(xbc_ref, z_ref, sm_ref, conv0_ref, ssm0_ref, cw_ref, cb_ref, dtb_ref, alog_ref, dsk_ref,
              ng_ref, tri_ref, exp_ref,
              y_ref, convo_ref, ssmo_ref,
              xp_ref, st_ref, act_ref, dte_ref, acse_ref, *, lb, valid, nc):
    c = pl.program_id(1)
    q = SSD_CHUNK

    @pl.when(c == 0)
    def _():
        if lb < q:
            xp_ref[...] = jnp.zeros_like(xp_ref)
        xp_ref[XP_BASE - 3:XP_BASE, :] = conv0_ref[...]
        for g in range(SSD_N_GROUPS):
            st_ref[g] = ssm0_ref[g * SSD_GROUP_W:(g + 1) * SSD_GROUP_W, :].T

    xp_ref[XP_BASE:XP_BASE + lb, :] = xbc_ref[...]

    cw = 512
    for j in range(SSD_CONV_DIM // cw):
        sl = slice(j * cw, (j + 1) * cw)
        conv = cb_ref[:, sl]
        for t in range(SSD_CONV_W):
            conv = conv + cw_ref[t:t + 1, sl] * xp_ref[XP_BASE - 3 + t:XP_BASE - 3 + t + q, sl]
        act_ref[:, sl] = _silu(conv)

    @pl.when(c == nc - 1)
    def _():
        convo_ref[...] = xp_ref[XP_BASE + valid - 3:XP_BASE + valid, :]

    xp_ref[XP_BASE - 3:XP_BASE, :] = xp_ref[XP_BASE + q - 3:XP_BASE + q, :]

    sm = sm_ref[...]
    if lb < q:
        sm = jnp.concatenate([sm, jnp.zeros((q - lb, LANES), F32)], axis=0)
    row = lax.broadcasted_iota(I32, (q, LANES), 0)
    lane = lax.broadcasted_iota(I32, (q, LANES), 1)
    xdt = sm + dtb_ref[...]
    dt = jnp.maximum(xdt, 0.0) + jnp.log1p(jnp.exp(-jnp.abs(xdt)))
    dt = jnp.where((lane < SSD_N_HEADS) & (row < valid), dt, 0.0)
    a = -jnp.exp(alog_ref[...])
    da = jnp.where(lane < SSD_N_HEADS, dt * a, 0.0)
    tri = tri_ref[...]
    acs = sum(_dot(tri, p) for p in _split_bf16(da, 3))
    acs_t = acs.T
    expand = exp_ref[...]
    dte_ref[...] = sum(_dot(p, expand) for p in _split_bf16(dt, 2))
    acse_ref[...] = sum(_dot(p, expand) for p in _split_bf16(acs, 2))

    tril = row >= lane
    for g in range(SSD_N_GROUPS):
        sl = slice(g * SSD_GROUP_W, (g + 1) * SSD_GROUP_W)
        xs_g = act_ref[:, sl]
        bm_g = act_ref[:, SSD_D_INNER + g * SSD_D_STATE:SSD_D_INNER + (g + 1) * SSD_D_STATE]
        cm_g = act_ref[:, SSD_D_INNER + SSD_N_GROUPS * SSD_D_STATE + g * SSD_D_STATE:
                       SSD_D_INNER + SSD_N_GROUPS * SSD_D_STATE + (g + 1) * SSD_D_STATE]
        acse_g = acse_ref[:, sl]
        alast_g = acse_ref[q - 1:q, sl]
        xdt_g = xs_g * dte_ref[:, sl]
        xw_g = xdt_g * jnp.exp(alast_g - acse_g)
        cmb = cm_g.astype(BF16)
        bmb = bm_g.astype(BF16)
        cb = _dot_nt(cmb, bmb)
        ys = []
        for r in range(SSD_HEADS_PER_GROUP):
            h = g * SSD_HEADS_PER_GROUP + r
            col = jnp.broadcast_to(acs[:, h:h + 1], (q, q))
            rowv = acs_t[h:h + 1, :]
            lmat = jnp.where(tril, jnp.exp(col - rowv), 0.0) * cb
            ys.append(_dot(lmat.astype(BF16),
                           xdt_g[:, r * SSD_HEAD_DIM:(r + 1) * SSD_HEAD_DIM].astype(BF16)))
        y = jnp.concatenate(ys, axis=1)
        st = st_ref[g]
        y = y + _dot(cmb, st.astype(BF16)) * jnp.exp(acse_g)
        st_ref[g] = st * jnp.exp(alast_g) + _dot(bm_g.T.astype(BF16), xw_g.astype(BF16))
        y = y + dsk_ref[:, sl] * xs_g
        yz = y[:lb] * _silu(z_ref[:, sl])
        y_ref[:, sl] = yz * lax.rsqrt(jnp.mean(yz * yz, axis=-1, keepdims=True) + EPS) * ng_ref[:, sl]

    @pl.when(c == nc - 1)
    def _():
        for g in range(SSD_N_GROUPS):
            ssmo_ref[g * SSD_GROUP_W:(g + 1) * SSD_GROUP_W, :] = st_ref[g].T


def _ssd(p3, conv0, ssm0, cw, cb, dtb, alog, dsk, ng, tri, expand, *, lb, valid):
    b, l, _ = p3.shape
    nc = l // lb
    q = SSD_CHUNK
    const = lambda shape: pl.BlockSpec(shape, lambda i, c: (0,) * len(shape))
    return pl.pallas_call(
        functools.partial(_ssd_body, lb=lb, valid=valid, nc=nc),
        grid=(b, nc),
        in_specs=[
            pl.BlockSpec((None, lb, SSD_CONV_DIM), lambda i, c: (i, c, P_XBC // SSD_CONV_DIM)),
            pl.BlockSpec((None, lb, SSD_D_INNER), lambda i, c: (i, c, P_Z // SSD_D_INNER)),
            pl.BlockSpec((None, lb, LANES), lambda i, c: (i, c, P_SMALL // LANES)),
            pl.BlockSpec((None, SSD_CONV_W - 1, SSD_CONV_DIM), lambda i, c: (i, 0, 0)),
            pl.BlockSpec((None, SSD_D_INNER, SSD_D_STATE), lambda i, c: (i, 0, 0)),
            const((SSD_CONV_W, SSD_CONV_DIM)),
            const((1, SSD_CONV_DIM)),
            const((1, LANES)),
            const((1, LANES)),
            const((1, SSD_D_INNER)),
            const((1, SSD_D_INNER)),
            const((q, q)),
            const((LANES, SSD_D_INNER)),
        ],
        out_specs=[
            pl.BlockSpec((None, lb, SSD_D_INNER), lambda i, c: (i, c, 0)),
            pl.BlockSpec((None, SSD_CONV_W - 1, SSD_CONV_DIM), lambda i, c: (i, 0, 0)),
            pl.BlockSpec((None, SSD_D_INNER, SSD_D_STATE), lambda i, c: (i, 0, 0)),
        ],
        out_shape=[
            jax.ShapeDtypeStruct((b, l, SSD_D_INNER), F32),
            jax.ShapeDtypeStruct((b, SSD_CONV_W - 1, SSD_CONV_DIM), F32),
            jax.ShapeDtypeStruct((b, SSD_D_INNER, SSD_D_STATE), F32),
        ],
        scratch_shapes=[
            pltpu.VMEM((XP_BASE + q, SSD_CONV_DIM), F32),
            pltpu.VMEM((SSD_N_GROUPS, SSD_D_STATE, SSD_GROUP_W), F32),
            pltpu.VMEM((q, SSD_CONV_DIM), F32),
            pltpu.VMEM((q, SSD_D_INNER), F32),
            pltpu.VMEM((q, SSD_D_INNER), F32),
        ],
        compiler_params=_cparams(("parallel", "arbitrary")),
        name="ssd",
    )(p3, p3, p3, conv0, ssm0, cw, cb, dtb, alog, dsk, ng, tri, expand)


def _count(pred):
    return jnp.sum(jnp.where(pred, 1.0, 0.0), axis=-1, keepdims=True)


def _topk_mask(score, topk):
    rows, n = score.shape
    assert n % LANES == 0
    idx_bits = max(1, (n - 1).bit_length())
    score = score + 0.0
    bits = lax.bitcast_convert_type(score, I32)
    key = jnp.where(bits < 0, bits ^ jnp.int32(0x7FFFFFFF), bits)
    kf = float(topk)

    def vbody(i, t):
        cand = t + (jnp.int32(1) << (30 - i))
        return jnp.where(_count(key >= cand) >= kf, cand, t)

    thr = jnp.where(_count(key >= 0) >= kf, jnp.int32(0), jnp.int32(-(1 << 31)))
    thr = lax.fori_loop(0, 31, vbody, thr)
    gt = key > thr
    eq = key == thr
    need = kf - _count(gt)
    excess = jnp.max(_count(eq) - need) > 0.0
    idx = lax.broadcasted_iota(I32, score.shape, 1)

    def cut_ties():
        def ibody(i, t):
            cand = t + (jnp.int32(1) << (idx_bits - 1 - i))
            return jnp.where(_count(eq & (idx < cand)) < need, cand, t)
        return lax.fori_loop(0, idx_bits, ibody, jnp.zeros((rows, 1), I32))

    cut = lax.cond(excess, cut_ties, lambda: jnp.full((rows, 1), n, I32))
    return gt | (eq & (idx <= cut))


IDX_DOT_SCALE = IDX_HEAD_DIM ** -0.5
IDX_W_SCALE = IDX_N_HEADS ** -0.5
ATT_SCALE = ATT_HEAD_DIM ** -0.5


CAUSAL_BRANCHES = 4
V_AUG_ROWS = ATT_HEAD_DIM + 16
LOG2_E = math.log2(math.e)


def _dsa_prompt_block(qb, nk, q_ref, qi_ref, sm_ref, o_ref, kb_ref, vb_ref, kib_ref, topk):
    qi = qi_ref[...]
    wi = sm_ref[:, SM_WI:SM_WI + IDX_N_HEADS] * (IDX_W_SCALE * IDX_DOT_SCALE)
    kib = kib_ref[:, :nk]
    score = jnp.zeros((Q_BLOCK, nk), F32)
    for h in range(IDX_N_HEADS):
        d = _dot(qi[:, h * IDX_HEAD_DIM:(h + 1) * IDX_HEAD_DIM].astype(BF16), kib)
        score = score + wi[:, h:h + 1] * jnp.maximum(d, 0.0)
    qpos = qb * Q_BLOCK + lax.broadcasted_iota(I32, (Q_BLOCK, nk), 0)
    kpos = lax.broadcasted_iota(I32, (Q_BLOCK, nk), 1)
    allowed = kpos <= qpos
    score = jnp.where(allowed, score, -jnp.inf)
    mask = _topk_mask(score, topk) & allowed

    q = (q_ref[...] * (ATT_SCALE * LOG2_E)).astype(BF16)
    outs = []
    for g in range(ATT_N_KV_HEADS):
        kg = kb_ref[g * ATT_HEAD_DIM:(g + 1) * ATT_HEAD_DIM, :nk]
        vg = vb_ref[g, :, :nk]
        for pair in range(ATT_REP // 2):
            ps = []
            for e in range(2):
                hh = g * ATT_REP + pair * 2 + e
                s = _dot(q[:, hh * ATT_HEAD_DIM:(hh + 1) * ATT_HEAD_DIM], kg)
                s = jnp.where(mask, s, -jnp.inf)
                ps.append(jnp.exp2(s - jnp.max(s, axis=-1, keepdims=True)).astype(BF16))
            ot = _dot_nt(vg, jnp.concatenate(ps, axis=0))
            o = (ot[0:ATT_HEAD_DIM] / ot[ATT_HEAD_DIM:ATT_HEAD_DIM + 1]).T
            outs.append(o[0:Q_BLOCK])
            outs.append(o[Q_BLOCK:2 * Q_BLOCK])
    o_ref[...] = jnp.concatenate(outs, axis=1)


def _dsa_prompt_body(q_ref, qi_ref, sm_ref, kt_ref, vt_ref, kit_ref, o_ref, kb_ref, vb_ref, kib_ref,
                     *, topk, key_lens, blocks_per_len):
    qb = pl.program_id(1)

    @pl.when(qb == 0)
    def _():
        kb_ref[...] = kt_ref[...].astype(BF16)
        row = lax.broadcasted_iota(I32, (V_AUG_ROWS - ATT_HEAD_DIM, vb_ref.shape[-1]), 0)
        tail = jnp.where(row == 0, 1.0, 0.0).astype(BF16)
        for g in range(ATT_N_KV_HEADS):
            vb_ref[g, 0:ATT_HEAD_DIM, :] = vt_ref[g * ATT_HEAD_DIM:(g + 1) * ATT_HEAD_DIM, :].astype(BF16)
            vb_ref[g, ATT_HEAD_DIM:V_AUG_ROWS, :] = tail
        kib_ref[...] = kit_ref[...].astype(BF16)

    for i, nk in enumerate(key_lens):
        @pl.when(qb // blocks_per_len == i)
        def _(nk=nk):
            _dsa_prompt_block(qb, nk, q_ref, qi_ref, sm_ref, o_ref, kb_ref, vb_ref, kib_ref, topk)


def _dsa_prompt(p3, kt, vt, kit):
    b, l, _ = p3.shape
    topk = min(TOPK_MAX, l // 4)
    aw = ATT_N_HEADS * ATT_HEAD_DIM
    qiw = IDX_N_HEADS * IDX_HEAD_DIM
    nblk = l // Q_BLOCK
    branches = math.gcd(nblk, CAUSAL_BRANCHES)
    blocks_per_len = nblk // branches
    key_lens = tuple((i + 1) * blocks_per_len * Q_BLOCK for i in range(branches))
    return pl.pallas_call(
        functools.partial(_dsa_prompt_body, topk=topk, key_lens=key_lens,
                          blocks_per_len=blocks_per_len),
        grid=(b, nblk),
        in_specs=[
            pl.BlockSpec((None, Q_BLOCK, aw), lambda i, j: (i, j, P_Q // aw)),
            pl.BlockSpec((None, Q_BLOCK, qiw), lambda i, j: (i, j, P_QI // qiw)),
            pl.BlockSpec((None, Q_BLOCK, LANES), lambda i, j: (i, j, P_SMALL // LANES)),
            pl.BlockSpec((None, ATT_KV_W, l), lambda i, j: (i, 0, 0)),
            pl.BlockSpec((None, ATT_KV_W, l), lambda i, j: (i, 0, 0)),
            pl.BlockSpec((None, IDX_HEAD_DIM, l), lambda i, j: (i, 0, 0)),
        ],
        out_specs=pl.BlockSpec((None, Q_BLOCK, aw), lambda i, j: (i, j, 0)),
        out_shape=jax.ShapeDtypeStruct((b, l, aw), F32),
        scratch_shapes=[
            pltpu.VMEM((ATT_KV_W, l), BF16),
            pltpu.VMEM((ATT_N_KV_HEADS, V_AUG_ROWS, l), BF16),
            pltpu.VMEM((IDX_HEAD_DIM, l), BF16),
        ],
        compiler_params=_cparams(("parallel", "arbitrary")),
        name="dsa_prompt",
    )(p3, p3, p3, kt, vt, kit)


SCORE_KEYS = 2048


def _dsa_sample_score_body(pt_ref, qi_ref, sm_ref, kidx_hbm, o_ref, kbuf, sem, *, n_pages, valid):
    b = pl.program_id(0)
    nb = pl.num_programs(0)
    past = n_pages * PAGE_SIZE

    def page_copy(bb, pg, slot):
        return pltpu.make_async_copy(kidx_hbm.at[pt_ref[bb, pg]],
                                     kbuf.at[slot, :, pl.ds(pg * PAGE_SIZE, PAGE_SIZE)],
                                     sem.at[slot])

    def fetch(bb, slot):
        def body(pg, carry):
            page_copy(bb, pg, slot).start()
            return carry
        lax.fori_loop(0, n_pages, body, 0)

    @pl.when(b == 0)
    def _():
        fetch(0, 0)

    @pl.when(b + 1 < nb)
    def _():
        fetch(b + 1, (b + 1) % 2)

    slot = b % 2

    def wbody(pg, carry):
        page_copy(b, pg, slot).wait()
        return carry
    lax.fori_loop(0, n_pages, wbody, 0)

    qi = qi_ref[...]
    qh = jnp.concatenate([qi[:, h * IDX_HEAD_DIM:(h + 1) * IDX_HEAD_DIM] for h in range(IDX_N_HEADS)],
                         axis=0).astype(BF16)
    wi = sm_ref[:, SM_WI:SM_WI + IDX_N_HEADS]
    wcol = (jnp.concatenate([wi[:, h:h + 1] for h in range(IDX_N_HEADS)], axis=0)
            * (IDX_W_SCALE * IDX_DOT_SCALE))

    def head_sum(d):
        w = wcol * jnp.maximum(d, 0.0)
        sc = w[0:SAMPLE_PAD]
        for h in range(1, IDX_N_HEADS):
            sc = sc + w[h * SAMPLE_PAD:(h + 1) * SAMPLE_PAD]
        return sc

    for kc in range(past // SCORE_KEYS):
        kk = kbuf[slot, :, kc * SCORE_KEYS:(kc + 1) * SCORE_KEYS].astype(BF16)
        sc = head_sum(_dot(qh, kk))
        o_ref[:, kc * SCORE_KEYS:(kc + 1) * SCORE_KEYS] = sc[0:SAMPLE_ROWS]

    kin = sm_ref[:, SM_KI:SM_KI + IDX_HEAD_DIM]
    kin = jnp.concatenate([kin, jnp.zeros((LANES - SAMPLE_PAD, IDX_HEAD_DIM), F32)], axis=0).astype(BF16)
    sc = head_sum(_dot_nt(qh, kin))
    trow = lax.broadcasted_iota(I32, (SAMPLE_PAD, LANES), 0)
    tcol = lax.broadcasted_iota(I32, (SAMPLE_PAD, LANES), 1)
    sc = jnp.where((tcol <= trow) & (tcol < valid), sc, -jnp.inf)
    o_ref[:, past:past + LANES] = sc[0:SAMPLE_ROWS]


def _dsa_sample_score(p3, kidx_pool, page_table, *, valid):
    b = p3.shape[0]
    n_pages = page_table.shape[1]
    past = n_pages * PAGE_SIZE
    qiw = IDX_N_HEADS * IDX_HEAD_DIM
    grid_spec = pltpu.PrefetchScalarGridSpec(
        num_scalar_prefetch=1,
        grid=(b,),
        in_specs=[
            pl.BlockSpec((None, SAMPLE_PAD, qiw), lambda i, pt: (i, 0, P_QI // qiw)),
            pl.BlockSpec((None, SAMPLE_PAD, LANES), lambda i, pt: (i, 0, P_SMALL // LANES)),
            pl.BlockSpec(memory_space=pl.ANY),
        ],
        out_specs=pl.BlockSpec((None, SAMPLE_ROWS, past + LANES), lambda i, pt: (i, 0, 0)),
        scratch_shapes=[
            pltpu.VMEM((2, IDX_HEAD_DIM, past), F32),
            pltpu.SemaphoreType.DMA((2,)),
        ],
    )
    return pl.pallas_call(
        functools.partial(_dsa_sample_score_body, n_pages=n_pages, valid=valid),
        grid_spec=grid_spec,
        out_shape=jax.ShapeDtypeStruct((b, SAMPLE_ROWS, past + LANES), F32),
        compiler_params=_cparams(("arbitrary",)),
        name="dsa_sample_score",
    )(page_table, p3, p3, kidx_pool)


def _select_body(s_ref, o_ref, *, topk):
    score = s_ref[...]
    mask = _topk_mask(score, topk) & (score > -jnp.inf)
    o_ref[...] = jnp.where(mask, 1.0, 0.0)


def _select(score2d, *, topk, tr):
    r, n = score2d.shape
    return pl.pallas_call(
        functools.partial(_select_body, topk=topk),
        grid=(r // tr,),
        in_specs=[pl.BlockSpec((tr, n), lambda i: (i, 0))],
        out_specs=pl.BlockSpec((tr, n), lambda i: (i, 0)),
        out_shape=jax.ShapeDtypeStruct((r, n), F32),
        compiler_params=_cparams(("parallel",)),
        name="dsa_sample_select",
    )(score2d)


ATT_ROWS = ATT_N_HEADS * SAMPLE_ROWS


def _dsa_sample_attn_body(pt_ref, q_ref, kn_ref, vn_ref, mask_ref, maskn_ref, kpool, vpool, o_ref,
                          kbuf, vbuf, sem, qbd_ref, m_ref, l_ref, acc_ref, *, n_chunks, ppc):
    b = pl.program_id(0)
    c = pl.program_id(1)
    nb = pl.num_programs(0)
    step = b * n_chunks + c

    def page_copies(bb, cc, j, slot):
        pg = pt_ref[bb, cc * ppc + j]
        dst = pl.ds(j * PAGE_SIZE, PAGE_SIZE)
        return (pltpu.make_async_copy(kpool.at[pg], kbuf.at[slot, :, dst], sem.at[0, slot]),
                pltpu.make_async_copy(vpool.at[pg], vbuf.at[slot, :, dst], sem.at[1, slot]))

    def fetch(st, slot):
        bb = st // n_chunks
        cc = st % n_chunks
        for j in range(ppc):
            for cp in page_copies(bb, cc, j, slot):
                cp.start()

    @pl.when(step == 0)
    def _():
        fetch(0, 0)

    @pl.when(step + 1 < nb * n_chunks)
    def _():
        fetch(step + 1, (step + 1) % 2)

    slot = step % 2
    for j in range(ppc):
        for cp in page_copies(b, c, j, slot):
            cp.wait()

    @pl.when(c == 0)
    def _():
        q = q_ref[...]
        qbd_ref[...] = jnp.zeros_like(qbd_ref)
        for g in range(ATT_N_KV_HEADS):
            for r in range(ATT_REP):
                hh = g * ATT_REP + r
                r0 = (g * ATT_REP + r) * SAMPLE_ROWS
                qbd_ref[r0:r0 + SAMPLE_ROWS, g * ATT_HEAD_DIM:(g + 1) * ATT_HEAD_DIM] = (
                    q[0:SAMPLE_ROWS, hh * ATT_HEAD_DIM:(hh + 1) * ATT_HEAD_DIM])
        m_ref[...] = jnp.full_like(m_ref, NEG_BIG)
        l_ref[...] = jnp.zeros_like(l_ref)
        acc_ref[...] = jnp.zeros_like(acc_ref)

    qbd = (qbd_ref[...] * ATT_SCALE).astype(BF16)

    def update(kt, vt, msk8):
        s = _dot(qbd, kt)
        mk = jnp.tile(msk8, (ATT_N_HEADS, 1)) > 0.0
        m_old = m_ref[...]
        m_new = jnp.maximum(m_old, jnp.max(jnp.where(mk, s, NEG_BIG), axis=-1, keepdims=True))
        p = jnp.where(mk, jnp.exp(s - m_new), 0.0)
        alpha = jnp.exp(m_old - m_new)
        l_ref[...] = alpha * l_ref[...] + jnp.sum(p, axis=-1, keepdims=True)
        acc_ref[...] = alpha * acc_ref[...] + _dot_nt(p.astype(BF16), vt)
        m_ref[...] = m_new

    update(kbuf[slot].astype(BF16), vbuf[slot].astype(BF16), mask_ref[...])

    @pl.when(c == n_chunks - 1)
    def _():
        pad = jnp.zeros((LANES - SAMPLE_PAD, ATT_KV_W), F32)
        kn = jnp.concatenate([kn_ref[...], pad], axis=0).T.astype(BF16)
        vn = jnp.concatenate([vn_ref[...], pad], axis=0).T.astype(BF16)
        update(kn, vn, maskn_ref[...])
        o = acc_ref[...] / l_ref[...]
        o_ref[...] = jnp.zeros_like(o_ref)
        for g in range(ATT_N_KV_HEADS):
            for r in range(ATT_REP):
                hh = g * ATT_REP + r
                r0 = (g * ATT_REP + r) * SAMPLE_ROWS
                o_ref[0:SAMPLE_ROWS, hh * ATT_HEAD_DIM:(hh + 1) * ATT_HEAD_DIM] = (
                    o[r0:r0 + SAMPLE_ROWS, g * ATT_HEAD_DIM:(g + 1) * ATT_HEAD_DIM])


def _dsa_sample_attn(p3, mask3, kpool, vpool, page_table, *, ppc=32):
    b = p3.shape[0]
    n_pages = page_table.shape[1]
    n_chunks = n_pages // ppc
    ck = ppc * PAGE_SIZE
    past = n_pages * PAGE_SIZE
    aw = ATT_N_HEADS * ATT_HEAD_DIM
    grid_spec = pltpu.PrefetchScalarGridSpec(
        num_scalar_prefetch=1,
        grid=(b, n_chunks),
        in_specs=[
            pl.BlockSpec((None, SAMPLE_PAD, aw), lambda i, c, pt: (i, 0, P_Q // aw)),
            pl.BlockSpec((None, SAMPLE_PAD, ATT_KV_W), lambda i, c, pt: (i, 0, P_K // ATT_KV_W)),
            pl.BlockSpec((None, SAMPLE_PAD, ATT_KV_W), lambda i, c, pt: (i, 0, P_V // ATT_KV_W)),
            pl.BlockSpec((None, SAMPLE_ROWS, ck), lambda i, c, pt: (i, 0, c)),
            pl.BlockSpec((None, SAMPLE_ROWS, LANES), lambda i, c, pt: (i, 0, past // LANES)),
            pl.BlockSpec(memory_space=pl.ANY),
            pl.BlockSpec(memory_space=pl.ANY),
        ],
        out_specs=pl.BlockSpec((None, SAMPLE_PAD, aw), lambda i, c, pt: (i, 0, 0)),
        scratch_shapes=[
            pltpu.VMEM((2, ATT_KV_W, ck), F32),
            pltpu.VMEM((2, ATT_KV_W, ck), F32),
            pltpu.SemaphoreType.DMA((2, 2)),
            pltpu.VMEM((ATT_ROWS, ATT_KV_W), F32),
            pltpu.VMEM((ATT_ROWS, 1), F32),
            pltpu.VMEM((ATT_ROWS, 1), F32),
            pltpu.VMEM((ATT_ROWS, ATT_KV_W), F32),
        ],
    )
    return pl.pallas_call(
        functools.partial(_dsa_sample_attn_body, n_chunks=n_chunks, ppc=ppc),
        grid_spec=grid_spec,
        out_shape=jax.ShapeDtypeStruct((b, SAMPLE_PAD, aw), F32),
        compiler_params=_cparams(("arbitrary", "arbitrary")),
        name="dsa_sample_attn",
    )(page_table, p3, p3, p3, mask3, mask3, kpool, vpool)


def _merge_body(h_ref, ys_ref, ya_ref, gs_ref, ga_ref, wbs_ref, wba_ref, wo_ref, pg_ref, o_ref):
    a = _dot(ys_ref[...].astype(BF16), wbs_ref[...])
    bb = _dot(ya_ref[...].astype(BF16), wba_ref[...])
    merged = _sigmoid(gs_ref[...]) * a + _sigmoid(ga_ref[...]) * bb
    o_ref[...] = h_ref[...] + _rms(_dot(merged.astype(BF16), wo_ref[...]), pg_ref[...])


def _merge(h, y_ssd, y_att, p2, wbs, wba, wo, post_g, *, tm):
    m = h.shape[0]
    const = lambda shape: pl.BlockSpec(shape, lambda i: (0,) * len(shape))
    return pl.pallas_call(
        _merge_body,
        grid=(m // tm,),
        in_specs=[
            pl.BlockSpec((tm, D_MODEL), lambda i: (i, 0)),
            pl.BlockSpec((tm, SSD_D_INNER), lambda i: (i, 0)),
            pl.BlockSpec((tm, D_MODEL), lambda i: (i, 0)),
            pl.BlockSpec((tm, D_MODEL), lambda i: (i, P_GSSD // D_MODEL)),
            pl.BlockSpec((tm, D_MODEL), lambda i: (i, P_GATT // D_MODEL)),
            const((SSD_D_INNER, D_MODEL)),
            const((D_MODEL, D_MODEL)),
            const((D_MODEL, D_MODEL)),
            const((1, D_MODEL)),
        ],
        out_specs=pl.BlockSpec((tm, D_MODEL), lambda i: (i, 0)),
        out_shape=jax.ShapeDtypeStruct((m, D_MODEL), F32),
        compiler_params=_cparams(("parallel",)),
        name="merge",
    )(h, y_ssd, y_att, p2, p2, wbs, wba, wo, post_g)


MEM_SCALE = MEM_HEAD_DIM ** -0.5


def _mem_body(h_ref, mk_ref, mv_ref, pg_ref, wq_ref, wo_ref, qg_ref, o_ref):
    h = h_ref[...]
    q = _dot(_rms(h, pg_ref[...]).astype(BF16), wq_ref[...])
    outs = []
    for hd in range(MEM_N_HEADS):
        sl = slice(hd * MEM_HEAD_DIM, (hd + 1) * MEM_HEAD_DIM)
        s = _dot_nt(q[:, sl].astype(BF16), mk_ref[:, sl].astype(BF16)) * MEM_SCALE
        p = jnp.exp(s - jnp.max(s, axis=-1, keepdims=True))
        den = jnp.sum(p, axis=-1, keepdims=True)
        outs.append(_dot(p.astype(BF16), mv_ref[:, sl].astype(BF16)) / den)
    att = jnp.concatenate(outs, axis=1)
    o_ref[...] = h + _rms(_dot(att.astype(BF16), wo_ref[...]), qg_ref[...])


def _mem_attn(h3, mk, mv, pre_g, wq, wo, post_g, *, tl):
    b, l, _ = h3.shape
    const = lambda shape: pl.BlockSpec(shape, lambda i, j: (0,) * len(shape))
    return pl.pallas_call(
        _mem_body,
        grid=(b, l // tl),
        in_specs=[
            pl.BlockSpec((None, tl, D_MODEL), lambda i, j: (i, j, 0)),
            pl.BlockSpec((None, MEM_LEN, D_MODEL), lambda i, j: (i, 0, 0)),
            pl.BlockSpec((None, MEM_LEN, D_MODEL), lambda i, j: (i, 0, 0)),
            const((1, D_MODEL)),
            const((D_MODEL, D_MODEL)),
            const((D_MODEL, D_MODEL)),
            const((1, D_MODEL)),
        ],
        out_specs=pl.BlockSpec((None, tl, D_MODEL), lambda i, j: (i, j, 0)),
        out_shape=jax.ShapeDtypeStruct((b, l, D_MODEL), F32),
        compiler_params=_cparams(("parallel", "parallel")),
        name="mem_attn",
    )(h3, mk, mv, pre_g, wq, wo, post_g)


def _prep_weights(w):
    wi = w["w_in"]
    offs = {}
    o = 0
    for name, width in (("z", SSD_D_INNER), ("xbc", SSD_CONV_DIM), ("dt", SSD_N_HEADS),
                        ("q", ATT_N_HEADS * ATT_HEAD_DIM), ("k", ATT_KV_W), ("v", ATT_KV_W),
                        ("qi", IDX_N_HEADS * IDX_HEAD_DIM), ("ki", IDX_HEAD_DIM), ("wi", IDX_N_HEADS),
                        ("gs", D_MODEL), ("ga", D_MODEL)):
        offs[name] = (o, o + width)
        o += width
    seg = lambda n: wi[:, offs[n][0]:offs[n][1]]
    pad = jnp.zeros((D_MODEL, P_WIDTH - P_SMALL - SSD_N_HEADS - IDX_HEAD_DIM - IDX_N_HEADS), wi.dtype)
    w_in = jnp.concatenate([seg("xbc"), seg("z"), seg("q"), seg("k"), seg("v"), seg("qi"), seg("gs"),
                            seg("ga"), seg("dt"), seg("ki"), seg("wi"), pad], axis=1).astype(BF16)
    w_kv_t = jnp.concatenate([seg("k"), seg("v"), seg("ki")], axis=1).T.astype(BF16)
    lane_pad = lambda v: jnp.pad(v, (0, LANES - v.shape[0]))[None, :]
    head = jnp.arange(SSD_D_INNER) // SSD_HEAD_DIM
    out = dict(w)
    out.update(
        w_in=w_in,
        w_kv_t=w_kv_t,
        dt_bias=lane_pad(w["dt_bias"]),
        a_log=lane_pad(w["a_log"]),
        d_skip=jnp.broadcast_to(w["d_skip"][:, None], (SSD_N_HEADS, SSD_HEAD_DIM)).reshape(1, SSD_D_INNER),
        tri=(jnp.arange(SSD_CHUNK)[:, None] >= jnp.arange(SSD_CHUNK)[None, :]).astype(BF16),
        expand=(jnp.arange(LANES)[:, None] == head[None, :]).astype(BF16),
    )
    steps = FFN_HIDDEN // FFN_TH
    for ffn in ("ffn1", "ffn2"):
        cols = lambda m: m.reshape(D_MODEL, steps, FFN_TH)
        out[ffn + "_wgu"] = jnp.concatenate([cols(w[ffn + "_wg"]), cols(w[ffn + "_wu"])],
                                            axis=2).transpose(1, 0, 2).astype(BF16)
    for name in ("ffn1_wd", "ffn2_wd", "w_br_ssd", "w_br_att", "w_out", "w_mq", "w_mk", "w_mv", "w_mo"):
        out[name] = w[name].astype(BF16)
    for name in ("ffn1_pre_g", "ffn1_post_g", "mix_pre_g", "mix_post_g", "mem_pre_g", "mem_kv_g",
                 "mem_post_g", "ffn2_pre_g", "ffn2_post_g", "conv_b", "ssd_norm_g"):
        out[name] = w[name][None, :]
    return out


def _layer(x3, w, conv0, ssm0, mem_k, mem_v, attn_fn, *, tm, tn, lb, valid, tl, kv_transposed):
    b, l, _ = x3.shape
    x = x3.reshape(b * l, D_MODEL)
    h = _ffn(x, w["ffn1_pre_g"], w["ffn1_wgu"], w["ffn1_wd"], w["ffn1_post_g"], tm=tm)
    if kv_transposed:
        p2, *kv_t = _norm_proj(h, w["mix_pre_g"], w["w_in"], tm=tm, tn=tn, wt=w["w_kv_t"], seq_len=l)
    else:
        p2, kv_t = _norm_proj(h, w["mix_pre_g"], w["w_in"], tm=tm, tn=tn), None
    p3 = p2.reshape(b, l, P_WIDTH)
    y_ssd, conv_new, ssm_new = _ssd(p3, conv0, ssm0, w["conv_w"], w["conv_b"], w["dt_bias"], w["a_log"],
                                    w["d_skip"], w["ssd_norm_g"], w["tri"], w["expand"], lb=lb, valid=valid)
    y_att = attn_fn(p3, *kv_t) if kv_transposed else attn_fn(p3)
    h = _merge(h, y_ssd.reshape(b * l, SSD_D_INNER), y_att.reshape(b * l, D_MODEL), p2,
               w["w_br_ssd"], w["w_br_att"], w["w_out"], w["mix_post_g"], tm=min(tm, 512))
    h = _mem_attn(h.reshape(b, l, D_MODEL), mem_k, mem_v, w["mem_pre_g"], w["w_mq"], w["w_mo"],
                  w["mem_post_g"], tl=tl).reshape(b * l, D_MODEL)
    h = _ffn(h, w["ffn2_pre_g"], w["ffn2_wgu"], w["ffn2_wd"], w["ffn2_post_g"], tm=tm)
    return h.reshape(b, l, D_MODEL), p3, kv_t, conv_new, ssm_new


def _kv_rows(p3, n):
    b = p3.shape[0]
    k = p3[:, :n, P_K:P_K + ATT_KV_W].reshape(1, b, n, ATT_N_KV_HEADS, ATT_HEAD_DIM)
    v = p3[:, :n, P_V:P_V + ATT_KV_W].reshape(1, b, n, ATT_N_KV_HEADS, ATT_HEAD_DIM)
    ki = p3[:, :n, P_SMALL + SM_KI:P_SMALL + SM_KI + IDX_HEAD_DIM][None]
    return k, v, ki


def _kv_rows_from_transposed(kt, vt, kit):
    b, _, l = kt.shape
    heads = lambda t: t.reshape(b, ATT_N_KV_HEADS, ATT_HEAD_DIM, l).transpose(0, 3, 1, 2)[None]
    return heads(kt), heads(vt), kit.transpose(0, 2, 1)[None]


def kernel(x_prompt, x_sample, mem_prompt, cache_k, cache_v, cache_kidx, state_ssm, state_conv, cache_mem_k, cache_mem_v, page_table, ffn1_pre_g, ffn1_wg, ffn1_wu, ffn1_wd, ffn1_post_g, mix_pre_g, w_in, conv_w, conv_b, dt_bias, a_log, d_skip, ssd_norm_g, w_br_ssd, w_br_att, w_out, mix_post_g, mem_pre_g, mem_kv_g, w_mq, w_mk, w_mv, w_mo, mem_post_g, ffn2_pre_g, ffn2_wg, ffn2_wu, ffn2_wd, ffn2_post_g):
    assert ffn1_wg.shape[0] == 1, "one trunk layer"
    w = _prep_weights(dict(
        ffn1_pre_g=ffn1_pre_g[0], ffn1_wg=ffn1_wg[0], ffn1_wu=ffn1_wu[0], ffn1_wd=ffn1_wd[0],
        ffn1_post_g=ffn1_post_g[0], mix_pre_g=mix_pre_g[0], w_in=w_in[0], conv_w=conv_w[0], conv_b=conv_b[0],
        dt_bias=dt_bias[0], a_log=a_log[0], d_skip=d_skip[0], ssd_norm_g=ssd_norm_g[0], w_br_ssd=w_br_ssd[0],
        w_br_att=w_br_att[0], w_out=w_out[0], mix_post_g=mix_post_g[0], mem_pre_g=mem_pre_g[0],
        mem_kv_g=mem_kv_g[0], w_mq=w_mq[0], w_mk=w_mk[0], w_mv=w_mv[0], w_mo=w_mo[0],
        mem_post_g=mem_post_g[0], ffn2_pre_g=ffn2_pre_g[0], ffn2_wg=ffn2_wg[0], ffn2_wu=ffn2_wu[0],
        ffn2_wd=ffn2_wd[0], ffn2_post_g=ffn2_post_g[0]))

    bp, lp, _ = x_prompt.shape
    mem2 = mem_prompt.reshape(bp * MEM_LEN, D_MODEL)
    mk_p = _norm_proj(mem2, w["mem_kv_g"], w["w_mk"], tm=512, tn=D_MODEL).reshape(bp, MEM_LEN, D_MODEL)
    mv_p = _norm_proj(mem2, w["mem_kv_g"], w["w_mv"], tm=512, tn=D_MODEL).reshape(bp, MEM_LEN, D_MODEL)
    conv0 = jnp.zeros((bp, SSD_CONV_W - 1, SSD_CONV_DIM), F32)
    ssm0 = jnp.zeros((bp, SSD_D_INNER, SSD_D_STATE), F32)
    y_p, _, kv_t, conv_p, ssm_p = _layer(x_prompt, w, conv0, ssm0, mk_p, mv_p, _dsa_prompt, tm=1024, tn=1152,
                                         lb=SSD_CHUNK, valid=SSD_CHUNK, tl=512, kv_transposed=True)
    k_p, v_p, ki_p = _kv_rows_from_transposed(*kv_t)

    bs, ls, _ = x_sample.shape
    assert ls <= SAMPLE_ROWS
    n_pages = page_table.shape[1]
    past = n_pages * PAGE_SIZE
    n_pool = cache_k.shape[1]
    kpool = cache_k[0].transpose(0, 2, 3, 1).reshape(n_pool, ATT_KV_W, PAGE_SIZE)
    vpool = cache_v[0].transpose(0, 2, 3, 1).reshape(n_pool, ATT_KV_W, PAGE_SIZE)
    kidx_pool = cache_kidx[0].transpose(0, 2, 1)
    topk_s = min(TOPK_MAX, (past + ls) // 4)

    def sample_attn(p3):
        score = _dsa_sample_score(p3, kidx_pool, page_table, valid=ls)
        mask = _select(score.reshape(bs * SAMPLE_ROWS, past + LANES), topk=topk_s, tr=32)
        return _dsa_sample_attn(p3, mask.reshape(bs, SAMPLE_ROWS, past + LANES), kpool, vpool, page_table)

    xs = jnp.pad(x_sample, ((0, 0), (0, SAMPLE_PAD - ls), (0, 0)))
    y_s, p3_s, _, conv_s, ssm_s = _layer(
        xs, w, state_conv[0], state_ssm[0].reshape(bs, SSD_D_INNER, SSD_D_STATE),
        cache_mem_k[0].reshape(bs, MEM_LEN, D_MODEL), cache_mem_v[0].reshape(bs, MEM_LEN, D_MODEL),
        sample_attn, tm=bs * SAMPLE_PAD, tn=1152, lb=SAMPLE_PAD, valid=ls, tl=SAMPLE_PAD,
        kv_transposed=False)
    k_s, v_s, ki_s = _kv_rows(p3_s, ls)

    ssm_shape = (1, -1, SSD_N_HEADS, SSD_HEAD_DIM, SSD_D_STATE)
    mem_shape = (1, bp, MEM_LEN, MEM_N_HEADS, MEM_HEAD_DIM)
    return (y_p, y_s[:, :ls], k_p, v_p, ki_p, ssm_p.reshape(ssm_shape), conv_p[None],
            mk_p.reshape(mem_shape), mv_p.reshape(mem_shape),
            k_s, v_s, ki_s, ssm_s.reshape(ssm_shape), conv_s[None])
```
